```python
import math
import jax, jax.numpy as jnp
from jax import lax
import numpy as np

D_MODEL = 2048
BATCH = 1
SEQ = 16384
DEPTH = 1

CHUNK = 64
Q_BLOCK = 128
N_MEM = 256

SB_HEADS = 8
SB_HEAD_DIM = 128
DSA_HEADS = 8
DSA_HEAD_DIM = 128
IDX_HEADS = 8
IDX_HEAD_DIM = 64
DSA_TOPK_MAX = 256
MEM_HEADS = 4
MEM_HEAD_DIM = 256

N_BRANCH = 3
SB_W = SB_HEADS * SB_HEAD_DIM
DSA_W = DSA_HEADS * DSA_HEAD_DIM
MEM_W = MEM_HEADS * MEM_HEAD_DIM
BRANCH_WIDTH = 1024
IDX_Q_W = IDX_HEADS * IDX_HEAD_DIM
GATE_W = N_BRANCH * D_MODEL
D_IN = 3 * SB_W + 3 * DSA_W + MEM_W + IDX_Q_W + IDX_HEAD_DIM + IDX_HEADS + GATE_W

D_FF = 5504
CONV_WIDTH = 3
ROPE_THETA = 10000.0
NORM_EPS = 1e-6

kernel_name = "hybrid_sb_dsa_mem_convffn_block"


def _rmsnorm(x, g):
    x32 = x.astype(jnp.float32)
    y = x32 * lax.rsqrt(jnp.mean(x32 * x32, axis=-1, keepdims=True) + NORM_EPS)
    return (y * g.astype(jnp.float32)).astype(x.dtype)


def _rope(x, positions):
    half = x.shape[-1] // 2
    inv = ROPE_THETA ** (-jnp.arange(half, dtype=jnp.float32) / half)
    ang = positions.astype(jnp.float32)[..., None] * inv
    cos = jnp.cos(ang)[:, :, None, :]
    sin = jnp.sin(ang)[:, :, None, :]
    x1 = x[..., :half].astype(jnp.float32)
    x2 = x[..., half:].astype(jnp.float32)
    out = jnp.concatenate([x1 * cos - x2 * sin, x2 * cos + x1 * sin], axis=-1)
    return out.astype(x.dtype)


def _to_blocks(a):
    B, S = a.shape[:2]
    return a.reshape((B, S // Q_BLOCK, Q_BLOCK) + a.shape[2:]).swapaxes(0, 1)


def _from_blocks(a):
    nb, B, qb, H, Dh = a.shape
    return a.swapaxes(0, 1).reshape(B, nb * qb, H * Dh)


def _stick_breaking(q, k, v):
    B, S, H, Dh = q.shape
    nb = S // Q_BLOCK
    starts = jnp.arange(nb, dtype=jnp.int32) * Q_BLOCK
    key_pos = jnp.arange(S, dtype=jnp.int32)
    scale = Dh ** -0.5

    def block(args):
        qb, t0 = args
        qpos = t0 + jnp.arange(Q_BLOCK, dtype=jnp.int32)
        visible = key_pos[None, :] < qpos[:, None]
        z = jnp.einsum('bqhd,bkhd->bhqk', qb, k).astype(jnp.float32) * scale
        sp = jnp.where(visible, jax.nn.softplus(z), 0.0)
        tail = lax.cumsum(sp, axis=3, reverse=True) - sp
        a = jnp.where(visible, jnp.exp(jax.nn.log_sigmoid(z) - tail), 0.0)
        return jnp.einsum('bhqk,bkhd->bqhd', a.astype(v.dtype), v)

    out = lax.map(block, (_to_blocks(q), starts))
    return _from_blocks(out)


def _dsa(q, k, v, q_idx, k_idx, w_idx, topk):
    B, S, H, Dh = q.shape
    nb = S // Q_BLOCK
    starts = jnp.arange(nb, dtype=jnp.int32) * Q_BLOCK
    key_chunk = jnp.arange(S, dtype=jnp.int32) // CHUNK
    idx_scale = IDX_HEAD_DIM ** -0.5
    scale = Dh ** -0.5

    def block(args):
        qb, qib, wib, t0 = args
        qchunk = (t0 + jnp.arange(Q_BLOCK, dtype=jnp.int32)) // CHUNK
        admissible = key_chunk[None, :] <= qchunk[:, None]
        logits = jnp.einsum('bqhd,bkd->bqhk', qib, k_idx).astype(jnp.float32) * idx_scale
        score = jnp.einsum('bqhk,bqh->bqk', jax.nn.relu(logits), wib.astype(jnp.float32))
        score = jnp.where(admissible[None], score, -jnp.inf)
        _, sel = lax.top_k(score, topk)
        valid = (sel // CHUNK) <= qchunk[None, :, None]
        k_sel = jax.vmap(lambda kk, ii: kk[ii])(k, sel)
        v_sel = jax.vmap(lambda vv, ii: vv[ii])(v, sel)
        s = jnp.einsum('bqhd,bqkhd->bhqk', qb, k_sel).astype(jnp.float32) * scale
        s = jnp.where(valid[:, None], s, -jnp.inf)
        p = jax.nn.softmax(s, axis=-1).astype(v.dtype)
        return jnp.einsum('bhqk,bqkhd->bqhd', p, v_sel)

    out = lax.map(block, (_to_blocks(q), _to_blocks(q_idx), _to_blocks(w_idx), starts))
    return _from_blocks(out)


def _mem_attn(q, mk, mv):
    B, S, H, Dh = q.shape
    s = jnp.einsum('bqhd,bmhd->bhqm', q, mk).astype(jnp.float32) * (Dh ** -0.5)
    p = jax.nn.softmax(s, axis=-1).astype(mv.dtype)
    return jnp.einsum('bhqm,bmhd->bqhd', p, mv).reshape(B, S, H * Dh)


def _causal_dwconv(u, w, b):
    S = u.shape[1]
    up = jnp.pad(u, ((0, 0), (CONV_WIDTH - 1, 0), (0, 0)))
    out = b
    for j in range(CONV_WIDTH):
        out = out + up[:, j:j + S] * w[j]
    return out


def _layer(x, mem, positions, attn_norm, mem_norm, w_in, b_gate, dsa_q_norm, dsa_k_norm,
           mem_q_norm, mem_k_norm, w_mem_kv, w_branch, w_out, ffn_norm, w_up, conv_w,
           conv_b, w_down, topk):
    B, S, D = x.shape
    h = _rmsnorm(x, attn_norm)
    proj = h @ w_in
    sizes = [SB_W, SB_W, SB_W, DSA_W, DSA_W, DSA_W, MEM_W, IDX_Q_W, IDX_HEAD_DIM,
             IDX_HEADS, GATE_W]
    cuts = [int(c) for c in np.cumsum(sizes)[:-1]]
    qa, ka, va, qb, kb, vb, qm, qi, ki, wi, gl = jnp.split(proj, cuts, axis=-1)

    sb_shape = (B, S, SB_HEADS, SB_HEAD_DIM)
    o_sb = _stick_breaking(qa.reshape(sb_shape), ka.reshape(sb_shape), va.reshape(sb_shape))

    dsa_shape = (B, S, DSA_HEADS, DSA_HEAD_DIM)
    qb = _rope(_rmsnorm(qb.reshape(dsa_shape), dsa_q_norm), positions)
    kb = _rope(_rmsnorm(kb.reshape(dsa_shape), dsa_k_norm), positions)
    vb = vb.reshape(dsa_shape)
    qi = _rope(qi.reshape(B, S, IDX_HEADS, IDX_HEAD_DIM), positions)
    ki = _rope(ki[:, :, None, :], positions)[:, :, 0, :]
    wi = wi * (IDX_HEADS ** -0.5)
    o_dsa = _dsa(qb, kb, vb, qi, ki, wi, topk)

    mh = _rmsnorm(mem, mem_norm)
    mk, mv = jnp.split(mh @ w_mem_kv, 2, axis=-1)
    mem_shape = (B, mem.shape[1], MEM_HEADS, MEM_HEAD_DIM)
    mk = _rmsnorm(mk.reshape(mem_shape), mem_k_norm)
    mv = mv.reshape(mem_shape)
    qm = _rmsnorm(qm.reshape(B, S, MEM_HEADS, MEM_HEAD_DIM), mem_q_norm)
    o_mem = _mem_attn(qm, mk, mv)

    gates = jax.nn.sigmoid((gl + b_gate).astype(jnp.float32)).astype(x.dtype)
    gates = gates.reshape(B, S, N_BRANCH, D)
    merged = (gates[:, :, 0] * (o_sb @ w_branch[0])
              + gates[:, :, 1] * (o_dsa @ w_branch[1])
              + gates[:, :, 2] * (o_mem @ w_branch[2]))
    x = x + merged @ w_out

    h2 = _rmsnorm(x, ffn_norm)
    u = _causal_dwconv(h2 @ w_up, conv_w, conv_b)
    g, val = jnp.split(u, 2, axis=-1)
    return x + (jax.nn.silu(g) * val) @ w_down


def setup_inputs(seed: int = 0) -> dict:
    key = jax.random.key(seed)
    ks = jax.random.split(key, 20)
    f32 = jnp.float32
    nrm = lambda k, shape, s: jax.random.normal(k, shape, f32) * s
    gain = lambda k, shape: 1.0 + 0.02 * jax.random.normal(k, shape, f32)
    offset = jax.random.randint(ks[2], (BATCH, 1), 0, 64, dtype=jnp.int32) * CHUNK
    positions = offset + jnp.arange(SEQ, dtype=jnp.int32)[None, :]
    return {
        "x": nrm(ks[0], (BATCH, SEQ, D_MODEL), 1.0),
        "mem": nrm(ks[1], (BATCH, N_MEM, D_MODEL), 1.0),
        "positions": positions,
        "attn_norm": gain(ks[3], (DEPTH, D_MODEL)),
        "mem_norm": gain(ks[4], (DEPTH, D_MODEL)),
        "w_in": nrm(ks[5], (DEPTH, D_MODEL, D_IN), D_MODEL ** -0.5),
        "b_gate": nrm(ks[6], (DEPTH, GATE_W), 0.02),
        "dsa_q_norm": gain(ks[7], (DEPTH, DSA_HEAD_DIM)),
        "dsa_k_norm": gain(ks[8], (DEPTH, DSA_HEAD_DIM)),
        "mem_q_norm": gain(ks[9], (DEPTH, MEM_HEAD_DIM)),
        "mem_k_norm": gain(ks[10], (DEPTH, MEM_HEAD_DIM)),
        "w_mem_kv": nrm(ks[11], (DEPTH, D_MODEL, 2 * MEM_W), D_MODEL ** -0.5),
        "w_branch": nrm(ks[12], (DEPTH, N_BRANCH, BRANCH_WIDTH, D_MODEL), BRANCH_WIDTH ** -0.5),
        "w_out": nrm(ks[13], (DEPTH, D_MODEL, D_MODEL), D_MODEL ** -0.5),
        "ffn_norm": gain(ks[14], (DEPTH, D_MODEL)),
        "w_up": nrm(ks[15], (DEPTH, D_MODEL, 2 * D_FF), D_MODEL ** -0.5),
        "conv_w": nrm(ks[16], (DEPTH, CONV_WIDTH, 2 * D_FF), CONV_WIDTH ** -0.5),
        "conv_b": nrm(ks[17], (DEPTH, 2 * D_FF), 0.02),
        "w_down": nrm(ks[18], (DEPTH, D_FF, D_MODEL), D_FF ** -0.5),
    }


def reference(x, mem, positions, attn_norm, mem_norm, w_in, b_gate, dsa_q_norm, dsa_k_norm,
              mem_q_norm, mem_k_norm, w_mem_kv, w_branch, w_out, ffn_norm, w_up, conv_w,
              conv_b, w_down):
    topk = min(DSA_TOPK_MAX, x.shape[1] // 4)
    for l in range(DEPTH):
        x = _layer(x, mem, positions, attn_norm[l], mem_norm[l], w_in[l], b_gate[l],
                   dsa_q_norm[l], dsa_k_norm[l], mem_q_norm[l], mem_k_norm[l],
                   w_mem_kv[l], w_branch[l], w_out[l], ffn_norm[l], w_up[l], conv_w[l],
                   conv_b[l], w_down[l], topk)
    return x
```

```python
import functools

import numpy as np
import jax
import jax.numpy as jnp
from jax import lax
from jax.experimental import pallas as pl
from jax.experimental.pallas import tpu as pltpu

F32 = jnp.float32
BF16 = jnp.bfloat16
I32 = jnp.int32

D_MODEL = 2048
CHUNK = 64
CHUNK_SHIFT = 6
HEADS = 8
HEAD_DIM = 128
IDX_HEAD_DIM = 64
TOPK_MAX = 256
MEM_HEADS = 4
MEM_HEAD_DIM = 256
D_FF = 5504
D_FF_PAD = 5632
ROPE_THETA = 10000.0
NORM_EPS = 1e-6
LANES = 128
VMEM_LIMIT = 56 * 1024 * 1024

NEG_BIG = -1e30
SB_TAIL_CUTOFF = 104.0


def _cparams(sem):
    return pltpu.CompilerParams(dimension_semantics=sem, vmem_limit_bytes=VMEM_LIMIT)


def _dot(a, b):
    return jnp.dot(a, b, preferred_element_type=F32)


def _dot_nt(a, b):
    return lax.dot_general(a, b, (((1,), (1,)), ((), ())), preferred_element_type=F32)


def _rmsnorm_body(x_ref, g_ref, o_ref):
    x = x_ref[...]
    ms = jnp.mean(x * x, axis=-1, keepdims=True)
    o_ref[...] = ((x * lax.rsqrt(ms + NORM_EPS)) * g_ref[...]).astype(o_ref.dtype)


def _rmsnorm(x, g, tm):
    m, d = x.shape
    return pl.pallas_call(
        _rmsnorm_body,
        out_shape=jax.ShapeDtypeStruct((m, d), BF16),
        grid=(m // tm,),
        in_specs=[pl.BlockSpec((tm, d), lambda i: (i, 0)),
                  pl.BlockSpec((1, d), lambda i: (0, 0))],
        out_specs=pl.BlockSpec((tm, d), lambda i: (i, 0)),
        compiler_params=_cparams(("parallel",)),
        name="rmsnorm",
    )(x, g.reshape(1, d))


def _epi_plain(y, o_ref):
    o_ref[...] = y.astype(o_ref.dtype)


def _head_rms(yh, g):
    ms = jnp.mean(yh * yh, axis=-1, keepdims=True)
    return (yh * lax.rsqrt(ms + NORM_EPS)) * g


def _epi_headnorm_rope(y, o_ref, g_ref, cos_ref, sin_ref):
    cos = cos_ref[...]
    sin = sin_ref[...]
    for h in range(y.shape[1] // HEAD_DIM):
        sl = slice(h * HEAD_DIM, (h + 1) * HEAD_DIM)
        yn = _head_rms(y[:, sl], g_ref[:, sl])
        o_ref[:, sl] = (yn * cos + pltpu.roll(yn, HEAD_DIM // 2, 1) * sin).astype(o_ref.dtype)


def _epi_headnorm_mem(y, o_ref, g_ref):
    for h in range(y.shape[1] // MEM_HEAD_DIM):
        sl = slice(h * MEM_HEAD_DIM, (h + 1) * MEM_HEAD_DIM)
        o_ref[:, sl] = _head_rms(y[:, sl], g_ref[:, sl]).astype(o_ref.dtype)


def _epi_rope_idx(y, o_ref, cos_ref, sin_ref):
    cos = cos_ref[...]
    sin = sin_ref[...]
    lane = lax.broadcasted_iota(I32, cos.shape, 1)
    first_half = (lane % IDX_HEAD_DIM) < (IDX_HEAD_DIM // 2)
    for t in range(y.shape[1] // LANES):
        sl = slice(t * LANES, (t + 1) * LANES)
        yt = y[:, sl]
        partner = jnp.where(first_half,
                            pltpu.roll(yt, LANES - IDX_HEAD_DIM // 2, 1),
                            pltpu.roll(yt, IDX_HEAD_DIM // 2, 1))
        o_ref[:, sl] = (yt * cos + partner * sin).astype(o_ref.dtype)


def _epi_scale(y, o_ref, *, scale):
    o_ref[...] = (y * scale).astype(o_ref.dtype)


def _epi_gate(y, o_ref, b_ref):
    o_ref[...] = jax.nn.sigmoid(y + b_ref[...]).astype(o_ref.dtype)


def _epi_residual(y, o_ref, r_ref):
    o_ref[...] = (r_ref[...] + y).astype(o_ref.dtype)


def _matmul_body(a_ref, w_ref, *rest, epi, n_extra, nk):
    extras = rest[:n_extra]
    o_ref = rest[n_extra]
    if nk == 1:
        epi(_dot(a_ref[...], w_ref[...]), o_ref, *extras)
        return
    acc_ref = rest[n_extra + 1]
    k = pl.program_id(2)

    @pl.when(k == 0)
    def _():
        acc_ref[...] = jnp.zeros_like(acc_ref)

    acc_ref[...] += _dot(a_ref[...], w_ref[...])

    @pl.when(k == nk - 1)
    def _():
        epi(acc_ref[...], o_ref, *extras)


def _matmul(a, w, epi, extras=(), extra_specs=(), *, out_dtype, tm, tn, tk=None, name):
    m, kdim = a.shape
    n = w.shape[1]
    tk = kdim if tk is None else tk
    nk = kdim // tk
    assert m % tm == 0 and n % tn == 0 and kdim % tk == 0
    scratch = [pltpu.VMEM((tm, tn), F32)] if nk > 1 else []
    return pl.pallas_call(
        functools.partial(_matmul_body, epi=epi, n_extra=len(extras), nk=nk),
        out_shape=jax.ShapeDtypeStruct((m, n), out_dtype),
        grid=(m // tm, n // tn, nk),
        in_specs=[pl.BlockSpec((tm, tk), lambda i, j, k: (i, k)),
                  pl.BlockSpec((tk, tn), lambda i, j, k: (k, j))] + list(extra_specs),
        out_specs=pl.BlockSpec((tm, tn), lambda i, j, k: (i, j)),
        scratch_shapes=scratch,
        compiler_params=_cparams(("parallel", "parallel", "arbitrary")),
        name=name,
    )(a, w, *extras)


def _row_spec(tm, width):
    return pl.BlockSpec((tm, width), lambda i, j, k: (i, 0))


def _col_spec(tn):
    return pl.BlockSpec((1, tn), lambda i, j, k: (0, j))


def _sb_body(q_ref, k_ref, v_ref, o_ref, acc_ref, r_ref, *, t, scale):
    i = pl.program_id(1)
    q = q_ref[...]
    row = lax.broadcasted_iota(I32, (t, t), 0)
    col = lax.broadcasted_iota(I32, (t, t), 1)
    upper = jnp.where(row > col, 1.0, 0.0).astype(BF16)
    visible = col < row

    def block(kb, diagonal):
        start = pl.multiple_of(kb * t, t)
        k = k_ref[pl.ds(start, t), :]
        v = v_ref[pl.ds(start, t), :]
        z = _dot_nt(q, k) * scale
        l1p = jnp.log1p(jnp.exp(-jnp.abs(z)))
        sp = jnp.maximum(z, 0.0) + l1p
        logsig = jnp.minimum(z, 0.0) - l1p
        if diagonal:
            sp = jnp.where(visible, sp, 0.0)
        hi = sp.astype(BF16)
        lo = (sp - hi.astype(F32)).astype(BF16)
        within = _dot(hi, upper) + _dot(lo, upper)
        r_old = r_ref[...]
        a = jnp.exp(logsig - (r_old + within))
        if diagonal:
            a = jnp.where(visible, a, 0.0)
        acc_ref[...] += _dot(a.astype(BF16), v)
        r_new = r_old + jnp.sum(sp, axis=-1, keepdims=True)
        r_ref[...] = r_new
        return jnp.min(r_new)

    acc_ref[...] = jnp.zeros_like(acc_ref)
    r_ref[...] = jnp.zeros_like(r_ref)
    r_min = block(i, True)

    def cond(c):
        kb, r_min = c
        return jnp.logical_and(kb >= 0, r_min < SB_TAIL_CUTOFF)

    def body(c):
        kb, _ = c
        return kb - 1, block(kb, False)

    lax.while_loop(cond, body, (i - 1, r_min))
    o_ref[...] = acc_ref[...].astype(o_ref.dtype)


def _stick_breaking(p, s, t):
    return pl.pallas_call(
        functools.partial(_sb_body, t=t, scale=HEAD_DIM ** -0.5),
        out_shape=jax.ShapeDtypeStruct((s, HEADS * HEAD_DIM), BF16),
        grid=(HEADS, s // t),
        in_specs=[pl.BlockSpec((t, HEAD_DIM), lambda h, i: (i, h)),
                  pl.BlockSpec((s, HEAD_DIM), lambda h, i: (0, HEADS + h)),
                  pl.BlockSpec((s, HEAD_DIM), lambda h, i: (0, 2 * HEADS + h))],
        out_specs=pl.BlockSpec((t, HEAD_DIM), lambda h, i: (i, h)),
        scratch_shapes=[pltpu.VMEM((t, HEAD_DIM), F32), pltpu.VMEM((t, 1), F32)],
        compiler_params=_cparams(("parallel", "parallel")),
        name="stick_breaking",
    )(p, p, p)


def _index_scores(qi_ref, kia, kib, wcols):
    score = None
    for pair in range(HEADS // 2):
        qp = qi_ref[:, pair * LANES:(pair + 1) * LANES]
        for half, kk in enumerate((kia, kib)):
            term = jnp.maximum(_dot_nt(qp, kk), 0.0) * wcols[2 * pair + half]
            score = term if score is None else score + term
    return jnp.where(score == 0.0, 0.0, score)


def _admissible(q0, k0, shape):
    qpos = q0 + lax.broadcasted_iota(I32, shape, 0)
    kpos = k0 + lax.broadcasted_iota(I32, shape, 1)
    return (kpos >> CHUNK_SHIFT) <= (qpos >> CHUNK_SHIFT), kpos


def _ordered_key(x):
    b = lax.bitcast_convert_type(x, I32)
    return b ^ ((b >> 31) & jnp.int32(0x7FFFFFFF))


KEY_NEG_INF = int(np.array(-np.inf, np.float32).view(np.int32) ^ 0x7FFFFFFF)
ROW_GROUP = 64


def _dsa_select_body(qi_ref, kia_ref, kib_ref, w_ref, thr_ref, jcut_ref, key_ref,
                     *, t, topk, s_total):
    i = pl.program_id(0)
    n_tiles = i + 1
    w = w_ref[...]
    wcols = [w[:, h:h + 1] for h in range(HEADS)]

    def fill(c, carry):
        start = pl.multiple_of(c * t, t)
        score = _index_scores(qi_ref, kia_ref[pl.ds(start, t), :],
                              kib_ref[pl.ds(start, t), :], wcols)
        adm, _ = _admissible(i * t, c * t, (t, t))
        key_ref[c] = _ordered_key(jnp.where(adm, score, -jnp.inf))
        return carry

    lax.fori_loop(0, n_tiles, fill, 0)

    lane_tiles = t // LANES
    col_in_tile = lax.broadcasted_iota(I32, (ROW_GROUP, LANES), 1)

    def count(r0, pred):
        def tile(c, cnt):
            keys = key_ref[c, pl.ds(r0, ROW_GROUP), :]
            for sub in range(lane_tiles):
                kcol = c * t + sub * LANES + col_in_tile
                hit = pred(keys[:, sub * LANES:(sub + 1) * LANES], kcol)
                cnt = cnt + jnp.where(hit, 1, 0).astype(I32)
            return cnt
        cnt = lax.fori_loop(0, n_tiles, tile, jnp.zeros((ROW_GROUP, LANES), I32))
        total = jnp.sum(cnt, axis=1, keepdims=True)
        return jnp.broadcast_to(total, (ROW_GROUP, LANES))

    def group(g, carry):
        r0 = pl.multiple_of(g * ROW_GROUP, ROW_GROUP)

        def bit_step(b, thr):
            cand = thr ^ (jnp.int32(1) << (31 - b))
            cnt = count(r0, lambda keys, kcol: keys >= cand)
            return jnp.where(cnt >= topk, cand, thr)

        thr = lax.fori_loop(0, 32, bit_step,
                            jnp.full((ROW_GROUP, LANES), -2 ** 31, I32))
        n_gt = count(r0, lambda keys, kcol: keys > thr)
        n_eq = count(r0, lambda keys, kcol: keys == thr)
        need = topk - n_gt
        tie = jnp.logical_and(n_eq > need, thr > KEY_NEG_INF)

        def tie_cut(_):
            def idx_step(b, p):
                cand = p | (jnp.int32(1) << (s_total.bit_length() - 1 - b))
                cnt = count(r0, lambda keys, kcol: jnp.logical_and(keys == thr, kcol < cand))
                return jnp.where(cnt < need, cand, p)
            p = lax.fori_loop(0, s_total.bit_length(), idx_step,
                              jnp.zeros((ROW_GROUP, LANES), I32))
            return jnp.where(tie, p, s_total)

        jcut = lax.cond(jnp.max(jnp.where(tie, 1.0, 0.0)) > 0.5, tie_cut,
                        lambda _: jnp.full((ROW_GROUP, LANES), s_total, I32), 0)
        bits = thr ^ ((thr >> 31) & jnp.int32(0x7FFFFFFF))
        thr_ref[pl.ds(r0, ROW_GROUP), :] = lax.bitcast_convert_type(bits, F32)
        jcut_ref[pl.ds(r0, ROW_GROUP), :] = jcut
        return carry

    lax.fori_loop(0, t // ROW_GROUP, group, 0)


def _dsa_select(qki, wi, s, t, topk):
    return pl.pallas_call(
        functools.partial(_dsa_select_body, t=t, topk=topk, s_total=s),
        out_shape=(jax.ShapeDtypeStruct((s, LANES), F32),
                   jax.ShapeDtypeStruct((s, LANES), I32)),
        grid=(s // t,),
        in_specs=[pl.BlockSpec((t, 4 * LANES), lambda i: (i, 0)),
                  pl.BlockSpec((s, LANES), lambda i: (0, 4)),
                  pl.BlockSpec((s, LANES), lambda i: (0, 5)),
                  pl.BlockSpec((t, LANES), lambda i: (i, 0))],
        out_specs=(pl.BlockSpec((t, LANES), lambda i: (i, 0)),
                   pl.BlockSpec((t, LANES), lambda i: (i, 0))),
        scratch_shapes=[pltpu.VMEM((s // t, t, t), I32)],
        compiler_params=_cparams(("parallel",)),
        name="dsa_select",
    )(qki, qki, qki, wi)


def _dsa_attn_body(qi_of, kj_of, q_ref, k_ref, v_ref, qi_ref, kia_ref, kib_ref, w_ref,
                   thr_ref, jcut_ref, o_ref, acc_ref, m_ref, l_ref, *, t, scale):
    step = pl.program_id(0)
    qi = qi_of[step]
    kj = kj_of[step]

    @pl.when(kj == 0)
    def _():
        acc_ref[...] = jnp.zeros_like(acc_ref)
        m_ref[...] = jnp.full_like(m_ref, NEG_BIG)
        l_ref[...] = jnp.zeros_like(l_ref)

    w = w_ref[...]
    wcols = [w[:, h:h + 1] for h in range(HEADS)]
    score = _index_scores(qi_ref, kia_ref[...], kib_ref[...], wcols)
    adm, kpos = _admissible(qi * t, kj * t, (t, t))
    thr = thr_ref[:, 0:1]
    jcut = jcut_ref[:, 0:1]
    chosen = jnp.logical_or(score > thr, jnp.logical_and(score == thr, kpos <= jcut))
    sel = jnp.logical_and(adm, chosen)

    for h in range(HEADS):
        sl = slice(h * HEAD_DIM, (h + 1) * HEAD_DIM)
        sc = jnp.where(sel, _dot_nt(q_ref[:, sl], k_ref[:, sl]) * scale, NEG_BIG)
        m_old = m_ref[h]
        m_new = jnp.maximum(m_old, jnp.max(sc, axis=-1, keepdims=True))
        p = jnp.where(sel, jnp.exp(sc - m_new), 0.0)
        alpha = jnp.exp(m_old - m_new)
        l_ref[h] = alpha * l_ref[h] + jnp.sum(p, axis=-1, keepdims=True)
        acc_ref[:, sl] = alpha * acc_ref[:, sl] + _dot(p.astype(BF16), v_ref[:, sl])
        m_ref[h] = m_new

    @pl.when(kj == qi)
    def _():
        for h in range(HEADS):
            sl = slice(h * HEAD_DIM, (h + 1) * HEAD_DIM)
            o_ref[:, sl] = (acc_ref[:, sl] / l_ref[h]).astype(o_ref.dtype)


def _dsa_attention(qkb, p, qki, wi, thr, jcut, s, t):
    nq = s // t
    qi_of = np.concatenate([np.full(q + 1, q, np.int32) for q in range(nq)])
    kj_of = np.concatenate([np.arange(q + 1, dtype=np.int32) for q in range(nq)])
    width = HEADS * HEAD_DIM
    grid_spec = pltpu.PrefetchScalarGridSpec(
        num_scalar_prefetch=2,
        grid=(len(qi_of),),
        in_specs=[
            pl.BlockSpec((t, width), lambda g, qo, ko: (qo[g], 0)),
            pl.BlockSpec((t, width), lambda g, qo, ko: (ko[g], 1)),
            pl.BlockSpec((t, width), lambda g, qo, ko: (ko[g], 3)),
            pl.BlockSpec((t, 4 * LANES), lambda g, qo, ko: (qo[g], 0)),
            pl.BlockSpec((t, LANES), lambda g, qo, ko: (ko[g], 4)),
            pl.BlockSpec((t, LANES), lambda g, qo, ko: (ko[g], 5)),
            pl.BlockSpec((t, LANES), lambda g, qo, ko: (qo[g], 0)),
            pl.BlockSpec((t, LANES), lambda g, qo, ko: (qo[g], 0)),
            pl.BlockSpec((t, LANES), lambda g, qo, ko: (qo[g], 0)),
        ],
        out_specs=pl.BlockSpec((t, width), lambda g, qo, ko: (qo[g], 0)),
        scratch_shapes=[pltpu.VMEM((t, width), F32),
                        pltpu.VMEM((HEADS, t, 1), F32),
                        pltpu.VMEM((HEADS, t, 1), F32)],
    )
    return pl.pallas_call(
        functools.partial(_dsa_attn_body, t=t, scale=HEAD_DIM ** -0.5),
        out_shape=jax.ShapeDtypeStruct((s, width), BF16),
        grid_spec=grid_spec,
        compiler_params=_cparams(("arbitrary",)),
        name="dsa_attention",
    )(jnp.asarray(qi_of), jnp.asarray(kj_of), qkb, qkb, p, qki, qki, qki, wi, thr, jcut)


def _mem_attn_body(q_ref, mk_ref, mv_ref, o_ref, *, scale):
    for h in range(MEM_HEADS):
        sl = slice(h * MEM_HEAD_DIM, (h + 1) * MEM_HEAD_DIM)
        sc = _dot_nt(q_ref[:, sl], mk_ref[:, sl]) * scale
        e = jnp.exp(sc - jnp.max(sc, axis=-1, keepdims=True))
        p = e / jnp.sum(e, axis=-1, keepdims=True)
        o_ref[:, sl] = _dot(p.astype(BF16), mv_ref[:, sl]).astype(o_ref.dtype)


def _mem_attention(qm, mk, mv, tm):
    s, width = qm.shape
    n_mem = mk.shape[0]
    return pl.pallas_call(
        functools.partial(_mem_attn_body, scale=MEM_HEAD_DIM ** -0.5),
        out_shape=jax.ShapeDtypeStruct((s, width), BF16),
        grid=(s // tm,),
        in_specs=[pl.BlockSpec((tm, width), lambda i: (i, 0)),
                  pl.BlockSpec((n_mem, width), lambda i: (0, 0)),
                  pl.BlockSpec((n_mem, width), lambda i: (0, 0))],
        out_specs=pl.BlockSpec((tm, width), lambda i: (i, 0)),
        compiler_params=_cparams(("parallel",)),
        name="mem_attention",
    )(qm, mk, mv)


def _merge_body(o0_ref, o1_ref, o2_ref, w_ref, g0_ref, g1_ref, g2_ref, out_ref):
    merged = (g0_ref[...] * _dot(o0_ref[...], w_ref[0])
              + g1_ref[...] * _dot(o1_ref[...], w_ref[1])
              + g2_ref[...] * _dot(o2_ref[...], w_ref[2]))
    out_ref[...] = merged.astype(out_ref.dtype)


def _merge(o_sb, o_dsa, o_mem, w_branch, gates, tm, tn):
    s, bw = o_sb.shape
    d = w_branch.shape[2]
    nj = d // tn
    o_spec = pl.BlockSpec((tm, bw), lambda i, j: (i, 0))
    gate_specs = [pl.BlockSpec((tm, tn), functools.partial(lambda i, j, b: (i, b * nj + j), b=b))
                  for b in range(3)]
    return pl.pallas_call(
        _merge_body,
        out_shape=jax.ShapeDtypeStruct((s, d), BF16),
        grid=(s // tm, nj),
        in_specs=[o_spec, o_spec, o_spec,
                  pl.BlockSpec((3, bw, tn), lambda i, j: (0, 0, j))] + gate_specs,
        out_specs=pl.BlockSpec((tm, tn), lambda i, j: (i, j)),
        compiler_params=_cparams(("parallel", "parallel")),
        name="gated_merge",
    )(o_sb, o_dsa, o_mem, w_branch, gates, gates, gates)


def _out_proj_body(m_ref, w_ref, x_ref, g_ref, x1_ref, h2_ref):
    x1 = x_ref[...] + _dot(m_ref[...], w_ref[...])
    x1_ref[...] = x1
    ms = jnp.mean(x1 * x1, axis=-1, keepdims=True)
    h2_ref[...] = ((x1 * lax.rsqrt(ms + NORM_EPS)) * g_ref[...]).astype(h2_ref.dtype)


def _out_proj(merged, w_out, x, g, tm):
    s, d = x.shape
    row = pl.BlockSpec((tm, d), lambda i: (i, 0))
    return pl.pallas_call(
        _out_proj_body,
        out_shape=(jax.ShapeDtypeStruct((s, d), F32), jax.ShapeDtypeStruct((s, d), BF16)),
        grid=(s // tm,),
        in_specs=[row, pl.BlockSpec((d, d), lambda i: (0, 0)), row,
                  pl.BlockSpec((1, d), lambda i: (0, 0))],
        out_specs=(row, row),
        compiler_params=_cparams(("parallel",)),
        name="out_proj",
    )(merged, w_out, x, g.reshape(1, d))


HALO = 8


def _ffn_up_body(h_ref, wg_ref, wv_ref, cwg_ref, cwv_ref, cbg_ref, cbv_ref, o_ref,
                 ug_ref, uv_ref, *, tm):
    i = pl.program_id(1)

    @pl.when(i == 0)
    def _():
        ug_ref[0:HALO, :] = jnp.zeros((HALO, ug_ref.shape[1]), F32)
        uv_ref[0:HALO, :] = jnp.zeros((HALO, uv_ref.shape[1]), F32)

    h = h_ref[...]
    ug_ref[HALO:HALO + tm, :] = _dot(h, wg_ref[...])
    uv_ref[HALO:HALO + tm, :] = _dot(h, wv_ref[...])

    def conv(u_ref, cw_ref, cb_ref):
        out = cb_ref[...] + u_ref[HALO - 2:HALO - 2 + tm, :] * cw_ref[0:1, :]
        out = out + u_ref[HALO - 1:HALO - 1 + tm, :] * cw_ref[1:2, :]
        return out + u_ref[HALO:HALO + tm, :] * cw_ref[2:3, :]

    g = conv(ug_ref, cwg_ref, cbg_ref)
    val = conv(uv_ref, cwv_ref, cbv_ref)
    o_ref[...] = (jax.nn.silu(g) * val).astype(o_ref.dtype)
    ug_ref[0:HALO, :] = ug_ref[tm:tm + HALO, :]
    uv_ref[0:HALO, :] = uv_ref[tm:tm + HALO, :]


def _ffn_up(h2, wg, wv, cwg, cwv, cbg, cbv, tm, tn):
    s, d = h2.shape
    n = wg.shape[1]
    wspec = pl.BlockSpec((d, tn), lambda j, i: (0, j))
    cwspec = pl.BlockSpec((3, tn), lambda j, i: (0, j))
    cbspec = pl.BlockSpec((1, tn), lambda j, i: (0, j))
    return pl.pallas_call(
        functools.partial(_ffn_up_body, tm=tm),
        out_shape=jax.ShapeDtypeStruct((s, n), BF16),
        grid=(n // tn, s // tm),
        in_specs=[pl.BlockSpec((tm, d), lambda j, i: (i, 0)),
                  wspec, wspec, cwspec, cwspec, cbspec, cbspec],
        out_specs=pl.BlockSpec((tm, tn), lambda j, i: (i, j)),
        scratch_shapes=[pltpu.VMEM((tm + HALO, tn), F32), pltpu.VMEM((tm + HALO, tn), F32)],
        compiler_params=_cparams(("parallel", "arbitrary")),
        name="ffn_up_conv_gate",
    )(h2, wg, wv, cwg, cwv, cbg, cbv)


def _rope_tables(positions, head_dim):
    half = head_dim // 2
    inv = ROPE_THETA ** (-jnp.arange(half, dtype=F32) / half)
    ang = positions.astype(F32)[:, None] * inv
    cos = jnp.cos(ang)
    sin = jnp.sin(ang)
    reps = LANES // head_dim
    cos_t = jnp.tile(jnp.concatenate([cos, cos], axis=1), (1, reps))
    sin_t = jnp.tile(jnp.concatenate([-sin, sin], axis=1), (1, reps))
    return cos_t, sin_t


def _pad_cols(a, n):
    return jnp.pad(a, ((0, 0), (0, n - a.shape[1])))


def _layer(x, mem, positions, attn_norm, mem_norm, w_in, b_gate, dsa_q_norm, dsa_k_norm,
           mem_q_norm, mem_k_norm, w_mem_kv, w_branch, w_out, ffn_norm, w_up, conv_w,
           conv_b, w_down):
    s, d = x.shape
    assert d == D_MODEL and s % 256 == 0
    topk = min(TOPK_MAX, s // 4)
    tm = min(1024, s)
    ta = 256
    hw = HEADS * HEAD_DIM

    c_qa, c_ka, c_va, c_qb, c_kb, c_vb, c_qm = (k * hw for k in range(7))
    c_qi = 7 * hw
    c_ki = c_qi + HEADS * IDX_HEAD_DIM
    c_wi = c_ki + IDX_HEAD_DIM
    c_gl = c_wi + HEADS
    wb16 = w_in.astype(BF16)
    zeros64 = jnp.zeros((d, IDX_HEAD_DIM), BF16)
    w_plain = jnp.concatenate([wb16[:, c_qa:c_qb], wb16[:, c_vb:c_qm]], axis=1)
    w_dsa_qk = wb16[:, c_qb:c_vb]
    w_qm = wb16[:, c_qm:c_qi]
    w_ki = wb16[:, c_ki:c_wi]
    w_idx = jnp.concatenate([wb16[:, c_qi:c_ki], w_ki, zeros64, zeros64, w_ki], axis=1)
    w_wi = _pad_cols(wb16[:, c_wi:c_gl], LANES)
    w_gate = wb16[:, c_gl:]

    cos_d, sin_d = _rope_tables(positions, HEAD_DIM)
    cos_i, sin_i = _rope_tables(positions, IDX_HEAD_DIM)

    h = _rmsnorm(x, attn_norm, min(512, s))

    p = _matmul(h, w_plain, _epi_plain, out_dtype=BF16, tm=tm, tn=1024, name="proj_plain")
    g_dsa = jnp.concatenate([jnp.tile(dsa_q_norm, HEADS), jnp.tile(dsa_k_norm, HEADS)])
    qkb = _matmul(h, w_dsa_qk, _epi_headnorm_rope,
                  (g_dsa.reshape(1, -1), cos_d, sin_d),
                  (_col_spec(1024), _row_spec(tm, LANES), _row_spec(tm, LANES)),
                  out_dtype=BF16, tm=tm, tn=1024, name="proj_dsa_qk")
    qm = _matmul(h, w_qm, _epi_headnorm_mem,
                 (jnp.tile(mem_q_norm, MEM_HEADS).reshape(1, -1),), (_col_spec(1024),),
                 out_dtype=BF16, tm=tm, tn=1024, name="proj_mem_q")
    qki = _matmul(h, w_idx, _epi_rope_idx, (cos_i, sin_i),
                  (_row_spec(tm, LANES), _row_spec(tm, LANES)),
                  out_dtype=BF16, tm=tm, tn=6 * LANES, name="proj_idx_qk")
    wi = _matmul(h, w_wi, functools.partial(_epi_scale, scale=HEADS ** -0.5 * IDX_HEAD_DIM ** -0.5),
                 out_dtype=F32, tm=tm, tn=LANES, name="proj_idx_w")
    gates = _matmul(h, w_gate, _epi_gate, (b_gate.reshape(1, -1),), (_col_spec(1024),),
                    out_dtype=F32, tm=tm, tn=1024, name="proj_gates")

    o_sb = _stick_breaking(p, s, ta)

    thr, jcut = _dsa_select(qki, wi, s, ta, topk)
    o_dsa = _dsa_attention(qkb, p, qki, wi, thr, jcut, s, ta)

    n_mem = mem.shape[0]
    mh = _rmsnorm(mem, mem_norm, n_mem)
    wkv16 = w_mem_kv.astype(BF16)
    mk = _matmul(mh, wkv16[:, :hw], _epi_headnorm_mem,
                 (jnp.tile(mem_k_norm, MEM_HEADS).reshape(1, -1),), (_col_spec(1024),),
                 out_dtype=BF16, tm=n_mem, tn=1024, name="proj_mem_k")
    mv = _matmul(mh, wkv16[:, hw:], _epi_plain, out_dtype=BF16, tm=n_mem, tn=1024,
                 name="proj_mem_v")
    o_mem = _mem_attention(qm, mk, mv, min(512, s))

    merged = _merge(o_sb, o_dsa, o_mem, w_branch.astype(BF16), gates, min(512, s), 1024)
    x1, h2 = _out_proj(merged, w_out.astype(BF16), x, ffn_norm, 256)

    wu16 = w_up.astype(BF16)
    wg = _pad_cols(wu16[:, :D_FF], D_FF_PAD)
    wv = _pad_cols(wu16[:, D_FF:], D_FF_PAD)
    cwg = _pad_cols(conv_w[:, :D_FF], D_FF_PAD)
    cwv = _pad_cols(conv_w[:, D_FF:], D_FF_PAD)
    cbg = _pad_cols(conv_b[None, :D_FF], D_FF_PAD)
    cbv = _pad_cols(conv_b[None, D_FF:], D_FF_PAD)
    act = _ffn_up(h2, wg, wv, cwg, cwv, cbg, cbv, tm, 512)
    wd = jnp.pad(w_down.astype(BF16), ((0, D_FF_PAD - D_FF), (0, 0)))
    return _matmul(act, wd, _epi_residual, (x1,),
                   (pl.BlockSpec((tm, 1024), lambda i, j, k: (i, j)),),
                   out_dtype=F32, tm=tm, tn=1024, tk=D_FF_PAD // 4, name="ffn_down")


def kernel(x, mem, positions, attn_norm, mem_norm, w_in, b_gate, dsa_q_norm, dsa_k_norm,
           mem_q_norm, mem_k_norm, w_mem_kv, w_branch, w_out, ffn_norm, w_up, conv_w,
           conv_b, w_down):
    assert x.shape[0] == 1 and attn_norm.shape[0] == 1
    y = _layer(x[0], mem[0], positions[0], attn_norm[0], mem_norm[0], w_in[0], b_gate[0],
               dsa_q_norm[0], dsa_k_norm[0], mem_q_norm[0], mem_k_norm[0], w_mem_kv[0],
               w_branch[0], w_out[0], ffn_norm[0], w_up[0], conv_w[0], conv_b[0], w_down[0])
    return y[None]
```

```python
import functools

import numpy as np
import jax
import jax.numpy as jnp
from jax import lax
from jax.experimental import pallas as pl
from jax.experimental.pallas import tpu as pltpu

F32 = jnp.float32
BF16 = jnp.bfloat16
I32 = jnp.int32

D_MODEL = 2048
CHUNK = 64
CHUNK_SHIFT = 6
HEADS = 8
HEAD_DIM = 128
IDX_HEAD_DIM = 64
TOPK_MAX = 256
MEM_HEADS = 4
MEM_HEAD_DIM = 256
D_FF = 5504
D_FF_PAD = 5632
ROPE_THETA = 10000.0
NORM_EPS = 1e-6
LANES = 128
VMEM_LIMIT = 56 * 1024 * 1024

LOG2_E = 1.4426950408889634
NEG_BIG = -1e30
BIAS_MASKED = -2e30
KEY_TILE = 512
SB_TAIL_CUTOFF = 104.0


def _cparams(sem):
    return pltpu.CompilerParams(dimension_semantics=sem, vmem_limit_bytes=VMEM_LIMIT)


def _dot(a, b):
    return jnp.dot(a, b, preferred_element_type=F32)


def _dot_nt(a, b):
    return lax.dot_general(a, b, (((1,), (1,)), ((), ())), preferred_element_type=F32)


def _rmsnorm_body(x_ref, g_ref, o_ref):
    x = x_ref[...]
    ms = jnp.mean(x * x, axis=-1, keepdims=True)
    o_ref[...] = ((x * lax.rsqrt(ms + NORM_EPS)) * g_ref[...]).astype(o_ref.dtype)


def _rmsnorm(x, g, tm):
    m, d = x.shape
    return pl.pallas_call(
        _rmsnorm_body,
        out_shape=jax.ShapeDtypeStruct((m, d), BF16),
        grid=(m // tm,),
        in_specs=[pl.BlockSpec((tm, d), lambda i: (i, 0)),
                  pl.BlockSpec((1, d), lambda i: (0, 0))],
        out_specs=pl.BlockSpec((tm, d), lambda i: (i, 0)),
        compiler_params=_cparams(("parallel",)),
        name="rmsnorm",
    )(x, g.reshape(1, d))


def _epi_plain(y, o_ref):
    o_ref[...] = y.astype(o_ref.dtype)


def _head_rms(yh, g):
    ms = jnp.mean(yh * yh, axis=-1, keepdims=True)
    return (yh * lax.rsqrt(ms + NORM_EPS)) * g


def _epi_headnorm_rope(y, o_ref, g_ref, cos_ref, sin_ref):
    cos = cos_ref[...]
    sin = sin_ref[...]
    for h in range(y.shape[1] // HEAD_DIM):
        sl = slice(h * HEAD_DIM, (h + 1) * HEAD_DIM)
        yn = _head_rms(y[:, sl], g_ref[:, sl])
        o_ref[:, sl] = (yn * cos + pltpu.roll(yn, HEAD_DIM // 2, 1) * sin).astype(o_ref.dtype)


def _epi_headnorm_mem(y, o_ref, g_ref):
    for h in range(y.shape[1] // MEM_HEAD_DIM):
        sl = slice(h * MEM_HEAD_DIM, (h + 1) * MEM_HEAD_DIM)
        o_ref[:, sl] = _head_rms(y[:, sl], g_ref[:, sl]).astype(o_ref.dtype)


def _epi_rope_idx(y, o_ref, cos_ref, sin_ref):
    cos = cos_ref[...]
    sin = sin_ref[...]
    lane = lax.broadcasted_iota(I32, cos.shape, 1)
    first_half = (lane % IDX_HEAD_DIM) < (IDX_HEAD_DIM // 2)
    for t in range(y.shape[1] // LANES):
        sl = slice(t * LANES, (t + 1) * LANES)
        yt = y[:, sl]
        partner = jnp.where(first_half,
                            pltpu.roll(yt, LANES - IDX_HEAD_DIM // 2, 1),
                            pltpu.roll(yt, IDX_HEAD_DIM // 2, 1))
        o_ref[:, sl] = (yt * cos + partner * sin).astype(o_ref.dtype)


def _epi_scale(y, o_ref, *, scale):
    o_ref[...] = (y * scale).astype(o_ref.dtype)


def _epi_gate(y, o_ref, b_ref):
    o_ref[...] = jax.nn.sigmoid(y + b_ref[...]).astype(o_ref.dtype)


def _epi_residual(y, o_ref, r_ref):
    o_ref[...] = (r_ref[...] + y).astype(o_ref.dtype)


def _matmul_body(a_ref, w_ref, *rest, epi, n_extra, nk):
    extras = rest[:n_extra]
    o_ref = rest[n_extra]
    if nk == 1:
        epi(_dot(a_ref[...], w_ref[...]), o_ref, *extras)
        return
    acc_ref = rest[n_extra + 1]
    k = pl.program_id(2)

    @pl.when(k == 0)
    def _():
        acc_ref[...] = jnp.zeros_like(acc_ref)

    acc_ref[...] += _dot(a_ref[...], w_ref[...])

    @pl.when(k == nk - 1)
    def _():
        epi(acc_ref[...], o_ref, *extras)


def _matmul(a, w, epi, extras=(), extra_specs=(), *, out_dtype, tm, tn, tk=None, name):
    m, kdim = a.shape
    n = w.shape[1]
    tk = kdim if tk is None else tk
    nk = kdim // tk
    assert m % tm == 0 and n % tn == 0 and kdim % tk == 0
    scratch = [pltpu.VMEM((tm, tn), F32)] if nk > 1 else []
    return pl.pallas_call(
        functools.partial(_matmul_body, epi=epi, n_extra=len(extras), nk=nk),
        out_shape=jax.ShapeDtypeStruct((m, n), out_dtype),
        grid=(m // tm, n // tn, nk),
        in_specs=[pl.BlockSpec((tm, tk), lambda i, j, k: (i, k)),
                  pl.BlockSpec((tk, tn), lambda i, j, k: (k, j))] + list(extra_specs),
        out_specs=pl.BlockSpec((tm, tn), lambda i, j, k: (i, j)),
        scratch_shapes=scratch,
        compiler_params=_cparams(("parallel", "parallel", "arbitrary")),
        name=name,
    )(a, w, *extras)


def _row_spec(tm, width):
    return pl.BlockSpec((tm, width), lambda i, j, k: (i, 0))


def _col_spec(tn):
    return pl.BlockSpec((1, tn), lambda i, j, k: (0, j))


def _sb_body(q_ref, k_ref, v_ref, o_ref, acc_ref, r_ref, *, t, scale):
    i = pl.program_id(1)
    q = q_ref[...]
    row = lax.broadcasted_iota(I32, (t, t), 0)
    col = lax.broadcasted_iota(I32, (t, t), 1)
    upper = jnp.where(row > col, 1.0, 0.0).astype(BF16)
    visible = col < row

    def block(kb, diagonal):
        start = pl.multiple_of(kb * t, t)
        k = k_ref[pl.ds(start, t), :]
        v = v_ref[pl.ds(start, t), :]
        z = _dot_nt(q, k) * scale
        l1p = jnp.log1p(jnp.exp(-jnp.abs(z)))
        sp = jnp.maximum(z, 0.0) + l1p
        logsig = jnp.minimum(z, 0.0) - l1p
        if diagonal:
            sp = jnp.where(visible, sp, 0.0)
        hi = sp.astype(BF16)
        lo = (sp - hi.astype(F32)).astype(BF16)
        within = _dot(hi, upper) + _dot(lo, upper)
        r_old = r_ref[...]
        a = jnp.exp(logsig - (r_old + within))
        if diagonal:
            a = jnp.where(visible, a, 0.0)
        acc_ref[...] += _dot(a.astype(BF16), v)
        r_new = r_old + jnp.sum(sp, axis=-1, keepdims=True)
        r_ref[...] = r_new
        return jnp.min(r_new)

    acc_ref[...] = jnp.zeros_like(acc_ref)
    r_ref[...] = jnp.zeros_like(r_ref)
    r_min = block(i, True)

    def cond(c):
        kb, r_min = c
        return jnp.logical_and(kb >= 0, r_min < SB_TAIL_CUTOFF)

    def body(c):
        kb, _ = c
        return kb - 1, block(kb, False)

    lax.while_loop(cond, body, (i - 1, r_min))
    o_ref[...] = acc_ref[...].astype(o_ref.dtype)


def _stick_breaking(p, s, t):
    return pl.pallas_call(
        functools.partial(_sb_body, t=t, scale=HEAD_DIM ** -0.5),
        out_shape=jax.ShapeDtypeStruct((s, HEADS * HEAD_DIM), BF16),
        grid=(HEADS, s // t),
        in_specs=[pl.BlockSpec((t, HEAD_DIM), lambda h, i: (i, h)),
                  pl.BlockSpec((s, HEAD_DIM), lambda h, i: (0, HEADS + h)),
                  pl.BlockSpec((s, HEAD_DIM), lambda h, i: (0, 2 * HEADS + h))],
        out_specs=pl.BlockSpec((t, HEAD_DIM), lambda h, i: (i, h)),
        scratch_shapes=[pltpu.VMEM((t, HEAD_DIM), F32), pltpu.VMEM((t, 1), F32)],
        compiler_params=_cparams(("parallel", "parallel")),
        name="stick_breaking",
    )(p, p, p)


def _split_index_heads(qi_ref):
    lane = lax.broadcasted_iota(I32, (qi_ref.shape[0], LANES), 1)
    halves = []
    for pair in range(HEADS // 2):
        qp = qi_ref[:, pair * LANES:(pair + 1) * LANES]
        zero = jnp.zeros_like(qp)
        halves.append(jnp.where(lane < IDX_HEAD_DIM, qp, zero))
        halves.append(jnp.where(lane >= IDX_HEAD_DIM, qp, zero))
    return halves


def _index_scores(q_halves, kk, wcols):
    score = None
    for h in range(HEADS):
        term = jnp.maximum(_dot_nt(q_halves[h], kk), 0.0) * wcols[h]
        score = term if score is None else score + term
    return jnp.where(score == 0.0, 0.0, score)


def _admissible(q0, k0, shape):
    qpos = q0 + lax.broadcasted_iota(I32, shape, 0)
    kpos = k0 + lax.broadcasted_iota(I32, shape, 1)
    return (kpos >> CHUNK_SHIFT) <= (qpos >> CHUNK_SHIFT), kpos


def _ordered_key(x):
    b = lax.bitcast_convert_type(x, I32)
    return b ^ ((b >> 31) & jnp.int32(0x7FFFFFFF))


KEY_NEG_INF = int(np.array(-np.inf, np.float32).view(np.int32) ^ 0x7FFFFFFF)
ROW_GROUP = 64


def _key_to_float(k):
    return lax.bitcast_convert_type(k ^ ((k >> 31) & jnp.int32(0x7FFFFFFF)), F32)


def _any(mask):
    return jnp.max(jnp.where(mask, 1.0, 0.0)) > 0.5


def _dsa_select_body(qi_ref, kk_ref, w_ref, bias_ref, key_ref, mx_ref, mn_ref,
                     *, t, topk, s_total):
    i = pl.program_id(0)
    n_tiles = (i * t + t + KEY_TILE - 1) // KEY_TILE
    n_all = s_total // KEY_TILE
    lane_tiles = KEY_TILE // LANES
    w = w_ref[...]
    wcols = [w[:, h:h + 1] for h in range(HEADS)]
    q_halves = _split_index_heads(qi_ref)
    mx_ref[...] = jnp.full(mx_ref.shape, -jnp.inf, F32)
    mn_ref[...] = jnp.full(mn_ref.shape, jnp.inf, F32)

    def fill(c, carry):
        start = pl.multiple_of(c * KEY_TILE, KEY_TILE)
        score = _index_scores(q_halves, kk_ref[pl.ds(start, KEY_TILE), :], wcols)
        adm, _ = _admissible(i * t, c * KEY_TILE, (t, KEY_TILE))
        low = jnp.where(adm, score, -jnp.inf)
        high = jnp.where(adm, score, jnp.inf)
        key_ref[c] = _ordered_key(low)
        mx = mx_ref[...]
        mn = mn_ref[...]
        for sub in range(lane_tiles):
            sl = slice(sub * LANES, (sub + 1) * LANES)
            mx = jnp.maximum(mx, low[:, sl])
            mn = jnp.minimum(mn, high[:, sl])
        mx_ref[...] = mx
        mn_ref[...] = mn
        return carry

    lax.fori_loop(0, n_tiles, fill, 0)

    shape = (ROW_GROUP, LANES)
    col_in_tile = lax.broadcasted_iota(I32, shape, 1)

    def count(r0, pred):
        def tile(c, cnt):
            keys = key_ref[c, pl.ds(r0, ROW_GROUP), :]
            for sub in range(lane_tiles):
                kcol = c * KEY_TILE + sub * LANES + col_in_tile
                hit = pred(keys[:, sub * LANES:(sub + 1) * LANES], kcol)
                cnt = cnt + jnp.where(hit, 1, 0).astype(I32)
            return cnt
        cnt = lax.fori_loop(0, n_tiles, tile, jnp.zeros(shape, I32))
        return jnp.broadcast_to(jnp.sum(cnt, axis=1, keepdims=True), shape)

    def group(g, carry):
        r0 = pl.multiple_of(g * ROW_GROUP, ROW_GROUP)
        rows = pl.ds(r0, ROW_GROUP)
        row_max = jnp.broadcast_to(jnp.max(mx_ref[rows, :], axis=1, keepdims=True), shape)
        row_min = jnp.broadcast_to(jnp.min(mn_ref[rows, :], axis=1, keepdims=True), shape)
        qpos = i * t + r0 + lax.broadcasted_iota(I32, shape, 0)
        n_adm = ((qpos >> CHUNK_SHIFT) + 1) * CHUNK
        enough = n_adm >= topk

        def active(lo, hi, c_lo):
            return jnp.logical_and(c_lo != topk, hi > lo + 1)

        def step(st):
            lo, hi, c_lo, c_hi, _ = st
            act = active(lo, hi, c_lo)
            mid = _ordered_key(0.5 * _key_to_float(lo) + 0.5 * _key_to_float(hi))
            key_mid = (lo >> 1) + (hi >> 1) + (lo & hi & 1)
            cand = jnp.where(jnp.logical_and(mid > lo, mid < hi), mid, key_mid)
            cnt = count(r0, lambda keys, kcol: keys >= cand)
            up = jnp.logical_and(act, cnt >= topk)
            down = jnp.logical_and(act, cnt < topk)
            lo = jnp.where(up, cand, lo)
            c_lo = jnp.where(up, cnt, c_lo)
            hi = jnp.where(down, cand, hi)
            c_hi = jnp.where(down, cnt, c_hi)
            return lo, hi, c_lo, c_hi, _any(active(lo, hi, c_lo))

        lo0 = jnp.where(enough, _ordered_key(row_min), KEY_NEG_INF)
        hi0 = _ordered_key(row_max) + 1
        c_lo0 = jnp.where(enough, n_adm, topk)
        st0 = (lo0, hi0, c_lo0, jnp.zeros(shape, I32), _any(active(lo0, hi0, c_lo0)))
        thr, _, c_lo, c_hi, _ = lax.while_loop(lambda st: st[4], step, st0)

        need = topk - c_hi
        tie = c_lo > topk

        def tie_cut(_):
            def idx_step(b, p):
                cand = p | (jnp.int32(1) << (s_total.bit_length() - 1 - b))
                cnt = count(r0, lambda keys, kcol: jnp.logical_and(keys == thr, kcol < cand))
                return jnp.where(cnt < need, cand, p)
            p = lax.fori_loop(0, s_total.bit_length(), idx_step, jnp.zeros(shape, I32))
            return jnp.where(tie, p, s_total)

        jcut = lax.cond(_any(tie), tie_cut, lambda _: jnp.full(shape, s_total, I32), 0)

        def emit(c, carry2):
            keys = key_ref[c, rows, :]
            parts = []
            for sub in range(lane_tiles):
                ksub = keys[:, sub * LANES:(sub + 1) * LANES]
                kcol = c * KEY_TILE + sub * LANES + col_in_tile
                chosen = jnp.logical_or(ksub > thr,
                                        jnp.logical_and(ksub == thr, kcol <= jcut))
                sel = jnp.logical_and(chosen, ksub > KEY_NEG_INF)
                parts.append(jnp.where(sel, 0.0, BIAS_MASKED))
            bias_ref[0, c, rows, :] = jnp.concatenate(parts, axis=1).astype(BF16)
            return carry2

        lax.fori_loop(0, n_tiles, emit, 0)

        def pad(c, carry2):
            bias_ref[0, c, rows, :] = jnp.full((ROW_GROUP, KEY_TILE), BIAS_MASKED, BF16)
            return carry2

        lax.fori_loop(n_tiles, n_all, pad, 0)
        return carry

    lax.fori_loop(0, t // ROW_GROUP, group, 0)


def _dsa_select(qki, wi, s, t, topk):
    n_all = s // KEY_TILE
    return pl.pallas_call(
        functools.partial(_dsa_select_body, t=t, topk=topk, s_total=s),
        out_shape=jax.ShapeDtypeStruct((s // t, n_all, t, KEY_TILE), BF16),
        grid=(s // t,),
        in_specs=[pl.BlockSpec((t, 4 * LANES), lambda i: (i, 0)),
                  pl.BlockSpec((s, LANES), lambda i: (0, 4)),
                  pl.BlockSpec((t, LANES), lambda i: (i, 0))],
        out_specs=pl.BlockSpec((1, n_all, t, KEY_TILE), lambda i: (i, 0, 0, 0)),
        scratch_shapes=[pltpu.VMEM((n_all, t, KEY_TILE), I32),
                        pltpu.VMEM((t, LANES), F32), pltpu.VMEM((t, LANES), F32)],
        compiler_params=_cparams(("parallel",)),
        name="dsa_select",
    )(qki, qki, wi)


def _dsa_attn_body(qi_of, kj_of, last_of, q_ref, k_ref, v_ref, bias_ref, o_ref,
                   acc_ref, m_ref, l_ref, *, scale):
    step = pl.program_id(0)

    @pl.when(kj_of[step] == 0)
    def _():
        acc_ref[...] = jnp.zeros_like(acc_ref)
        m_ref[...] = jnp.full_like(m_ref, NEG_BIG)
        l_ref[...] = jnp.zeros_like(l_ref)

    bias = bias_ref[0, 0].astype(F32)
    for h in range(HEADS):
        sl = slice(h * HEAD_DIM, (h + 1) * HEAD_DIM)
        z = _dot_nt(q_ref[:, sl], k_ref[:, sl]) * (scale * LOG2_E) + bias
        m_old = m_ref[h]
        m_new = jnp.maximum(m_old, jnp.max(z, axis=-1, keepdims=True))
        p = jnp.exp2(z - m_new)
        alpha = jnp.exp2(m_old - m_new)
        l_ref[h] = alpha * l_ref[h] + jnp.sum(p, axis=-1, keepdims=True)
        acc_ref[:, sl] = alpha * acc_ref[:, sl] + _dot(p.astype(BF16), v_ref[:, sl])
        m_ref[h] = m_new

    @pl.when(last_of[step] == 1)
    def _():
        for h in range(HEADS):
            sl = slice(h * HEAD_DIM, (h + 1) * HEAD_DIM)
            o_ref[:, sl] = (acc_ref[:, sl] / l_ref[h]).astype(o_ref.dtype)


def _dsa_attention(qkb, p, bias, s, t):
    qi_of, kj_of, last_of = [], [], []
    for q in range(s // t):
        n_tiles = (q * t + t + KEY_TILE - 1) // KEY_TILE
        qi_of += [q] * n_tiles
        kj_of += list(range(n_tiles))
        last_of += [0] * (n_tiles - 1) + [1]
    width = HEADS * HEAD_DIM
    grid_spec = pltpu.PrefetchScalarGridSpec(
        num_scalar_prefetch=3,
        grid=(len(qi_of),),
        in_specs=[
            pl.BlockSpec((t, width), lambda g, qo, ko, lo: (qo[g], 0)),
            pl.BlockSpec((KEY_TILE, width), lambda g, qo, ko, lo: (ko[g], 1)),
            pl.BlockSpec((KEY_TILE, width), lambda g, qo, ko, lo: (ko[g], 3)),
            pl.BlockSpec((1, 1, t, KEY_TILE), lambda g, qo, ko, lo: (qo[g], ko[g], 0, 0)),
        ],
        out_specs=pl.BlockSpec((t, width), lambda g, qo, ko, lo: (qo[g], 0)),
        scratch_shapes=[pltpu.VMEM((t, width), F32),
                        pltpu.VMEM((HEADS, t, 1), F32),
                        pltpu.VMEM((HEADS, t, 1), F32)],
    )
    to_i32 = lambda v: jnp.asarray(np.asarray(v, np.int32))
    return pl.pallas_call(
        functools.partial(_dsa_attn_body, scale=HEAD_DIM ** -0.5),
        out_shape=jax.ShapeDtypeStruct((s, width), BF16),
        grid_spec=grid_spec,
        compiler_params=_cparams(("arbitrary",)),
        name="dsa_attention",
    )(to_i32(qi_of), to_i32(kj_of), to_i32(last_of), qkb, qkb, p, bias)


def _mem_attn_body(q_ref, mk_ref, mv_ref, o_ref, *, scale):
    for h in range(MEM_HEADS):
        sl = slice(h * MEM_HEAD_DIM, (h + 1) * MEM_HEAD_DIM)
        sc = _dot_nt(q_ref[:, sl], mk_ref[:, sl]) * scale
        e = jnp.exp(sc - jnp.max(sc, axis=-1, keepdims=True))
        p = e / jnp.sum(e, axis=-1, keepdims=True)
        o_ref[:, sl] = _dot(p.astype(BF16), mv_ref[:, sl]).astype(o_ref.dtype)


def _mem_attention(qm, mk, mv, tm):
    s, width = qm.shape
    n_mem = mk.shape[0]
    return pl.pallas_call(
        functools.partial(_mem_attn_body, scale=MEM_HEAD_DIM ** -0.5),
        out_shape=jax.ShapeDtypeStruct((s, width), BF16),
        grid=(s // tm,),
        in_specs=[pl.BlockSpec((tm, width), lambda i: (i, 0)),
                  pl.BlockSpec((n_mem, width), lambda i: (0, 0)),
                  pl.BlockSpec((n_mem, width), lambda i: (0, 0))],
        out_specs=pl.BlockSpec((tm, width), lambda i: (i, 0)),
        compiler_params=_cparams(("parallel",)),
        name="mem_attention",
    )(qm, mk, mv)


def _merge_body(o0_ref, o1_ref, o2_ref, w_ref, g0_ref, g1_ref, g2_ref, out_ref):
    merged = (g0_ref[...] * _dot(o0_ref[...], w_ref[0])
              + g1_ref[...] * _dot(o1_ref[...], w_ref[1])
              + g2_ref[...] * _dot(o2_ref[...], w_ref[2]))
    out_ref[...] = merged.astype(out_ref.dtype)


def _merge(o_sb, o_dsa, o_mem, w_branch, gates, tm, tn):
    s, bw = o_sb.shape
    d = w_branch.shape[2]
    nj = d // tn
    o_spec = pl.BlockSpec((tm, bw), lambda i, j: (i, 0))
    gate_specs = [pl.BlockSpec((tm, tn), functools.partial(lambda i, j, b: (i, b * nj + j), b=b))
                  for b in range(3)]
    return pl.pallas_call(
        _merge_body,
        out_shape=jax.ShapeDtypeStruct((s, d), BF16),
        grid=(s // tm, nj),
        in_specs=[o_spec, o_spec, o_spec,
                  pl.BlockSpec((3, bw, tn), lambda i, j: (0, 0, j))] + gate_specs,
        out_specs=pl.BlockSpec((tm, tn), lambda i, j: (i, j)),
        compiler_params=_cparams(("parallel", "parallel")),
        name="gated_merge",
    )(o_sb, o_dsa, o_mem, w_branch, gates, gates, gates)


def _out_proj_body(m_ref, w_ref, x_ref, g_ref, x1_ref, h2_ref):
    x1 = x_ref[...] + _dot(m_ref[...], w_ref[...])
    x1_ref[...] = x1
    ms = jnp.mean(x1 * x1, axis=-1, keepdims=True)
    h2_ref[...] = ((x1 * lax.rsqrt(ms + NORM_EPS)) * g_ref[...]).astype(h2_ref.dtype)


def _out_proj(merged, w_out, x, g, tm):
    s, d = x.shape
    row = pl.BlockSpec((tm, d), lambda i: (i, 0))
    return pl.pallas_call(
        _out_proj_body,
        out_shape=(jax.ShapeDtypeStruct((s, d), F32), jax.ShapeDtypeStruct((s, d), BF16)),
        grid=(s // tm,),
        in_specs=[row, pl.BlockSpec((d, d), lambda i: (0, 0)), row,
                  pl.BlockSpec((1, d), lambda i: (0, 0))],
        out_specs=(row, row),
        compiler_params=_cparams(("parallel",)),
        name="out_proj",
    )(merged, w_out, x, g.reshape(1, d))


HALO = 8


def _ffn_up_body(h_ref, wg_ref, wv_ref, cwg_ref, cwv_ref, cbg_ref, cbv_ref, o_ref,
                 ug_ref, uv_ref, *, tm):
    i = pl.program_id(1)

    @pl.when(i == 0)
    def _():
        ug_ref[0:HALO, :] = jnp.zeros((HALO, ug_ref.shape[1]), F32)
        uv_ref[0:HALO, :] = jnp.zeros((HALO, uv_ref.shape[1]), F32)

    h = h_ref[...]
    ug_ref[HALO:HALO + tm, :] = _dot(h, wg_ref[...])
    uv_ref[HALO:HALO + tm, :] = _dot(h, wv_ref[...])

    def conv(u_ref, cw_ref, cb_ref):
        out = cb_ref[...] + u_ref[HALO - 2:HALO - 2 + tm, :] * cw_ref[0:1, :]
        out = out + u_ref[HALO - 1:HALO - 1 + tm, :] * cw_ref[1:2, :]
        return out + u_ref[HALO:HALO + tm, :] * cw_ref[2:3, :]

    g = conv(ug_ref, cwg_ref, cbg_ref)
    val = conv(uv_ref, cwv_ref, cbv_ref)
    o_ref[...] = (jax.nn.silu(g) * val).astype(o_ref.dtype)
    ug_ref[0:HALO, :] = ug_ref[tm:tm + HALO, :]
    uv_ref[0:HALO, :] = uv_ref[tm:tm + HALO, :]


def _ffn_up(h2, wg, wv, cwg, cwv, cbg, cbv, tm, tn):
    s, d = h2.shape
    n = wg.shape[1]
    wspec = pl.BlockSpec((d, tn), lambda j, i: (0, j))
    cwspec = pl.BlockSpec((3, tn), lambda j, i: (0, j))
    cbspec = pl.BlockSpec((1, tn), lambda j, i: (0, j))
    return pl.pallas_call(
        functools.partial(_ffn_up_body, tm=tm),
        out_shape=jax.ShapeDtypeStruct((s, n), BF16),
        grid=(n // tn, s // tm),
        in_specs=[pl.BlockSpec((tm, d), lambda j, i: (i, 0)),
                  wspec, wspec, cwspec, cwspec, cbspec, cbspec],
        out_specs=pl.BlockSpec((tm, tn), lambda j, i: (i, j)),
        scratch_shapes=[pltpu.VMEM((tm + HALO, tn), F32), pltpu.VMEM((tm + HALO, tn), F32)],
        compiler_params=_cparams(("parallel", "arbitrary")),
        name="ffn_up_conv_gate",
    )(h2, wg, wv, cwg, cwv, cbg, cbv)


def _rope_tables(positions, head_dim):
    half = head_dim // 2
    inv = ROPE_THETA ** (-jnp.arange(half, dtype=F32) / half)
    ang = positions.astype(F32)[:, None] * inv
    cos = jnp.cos(ang)
    sin = jnp.sin(ang)
    reps = LANES // head_dim
    cos_t = jnp.tile(jnp.concatenate([cos, cos], axis=1), (1, reps))
    sin_t = jnp.tile(jnp.concatenate([-sin, sin], axis=1), (1, reps))
    return cos_t, sin_t


def _pad_cols(a, n):
    return jnp.pad(a, ((0, 0), (0, n - a.shape[1])))


def _layer(x, mem, positions, attn_norm, mem_norm, w_in, b_gate, dsa_q_norm, dsa_k_norm,
           mem_q_norm, mem_k_norm, w_mem_kv, w_branch, w_out, ffn_norm, w_up, conv_w,
           conv_b, w_down):
    s, d = x.shape
    assert d == D_MODEL and s % KEY_TILE == 0
    topk = min(TOPK_MAX, s // 4)
    tm = min(1024, s)
    ta = 256
    hw = HEADS * HEAD_DIM

    c_qa, c_ka, c_va, c_qb, c_kb, c_vb, c_qm = (k * hw for k in range(7))
    c_qi = 7 * hw
    c_ki = c_qi + HEADS * IDX_HEAD_DIM
    c_wi = c_ki + IDX_HEAD_DIM
    c_gl = c_wi + HEADS
    wb16 = w_in.astype(BF16)
    w_plain = jnp.concatenate([wb16[:, c_qa:c_qb], wb16[:, c_vb:c_qm]], axis=1)
    w_dsa_qk = wb16[:, c_qb:c_vb]
    w_qm = wb16[:, c_qm:c_qi]
    w_ki = wb16[:, c_ki:c_wi]
    w_idx = jnp.concatenate([wb16[:, c_qi:c_ki], w_ki, w_ki], axis=1)
    w_wi = _pad_cols(wb16[:, c_wi:c_gl], LANES)
    w_gate = wb16[:, c_gl:]

    cos_d, sin_d = _rope_tables(positions, HEAD_DIM)
    cos_i, sin_i = _rope_tables(positions, IDX_HEAD_DIM)

    h = _rmsnorm(x, attn_norm, min(512, s))

    p = _matmul(h, w_plain, _epi_plain, out_dtype=BF16, tm=tm, tn=1024, name="proj_plain")
    g_dsa = jnp.concatenate([jnp.tile(dsa_q_norm, HEADS), jnp.tile(dsa_k_norm, HEADS)])
    qkb = _matmul(h, w_dsa_qk, _epi_headnorm_rope,
                  (g_dsa.reshape(1, -1), cos_d, sin_d),
                  (_col_spec(1024), _row_spec(tm, LANES), _row_spec(tm, LANES)),
                  out_dtype=BF16, tm=tm, tn=1024, name="proj_dsa_qk")
    qm = _matmul(h, w_qm, _epi_headnorm_mem,
                 (jnp.tile(mem_q_norm, MEM_HEADS).reshape(1, -1),), (_col_spec(1024),),
                 out_dtype=BF16, tm=tm, tn=1024, name="proj_mem_q")
    qki = _matmul(h, w_idx, _epi_rope_idx, (cos_i, sin_i),
                  (_row_spec(tm, LANES), _row_spec(tm, LANES)),
                  out_dtype=BF16, tm=tm, tn=5 * LANES, name="proj_idx_qk")
    wi = _matmul(h, w_wi, functools.partial(_epi_scale, scale=HEADS ** -0.5 * IDX_HEAD_DIM ** -0.5),
                 out_dtype=F32, tm=tm, tn=LANES, name="proj_idx_w")
    gates = _matmul(h, w_gate, _epi_gate, (b_gate.reshape(1, -1),), (_col_spec(1024),),
                    out_dtype=F32, tm=tm, tn=1024, name="proj_gates")

    o_sb = _stick_breaking(p, s, ta)

    bias = _dsa_select(qki, wi, s, ta, topk)
    o_dsa = _dsa_attention(qkb, p, bias, s, ta)

    n_mem = mem.shape[0]
    mh = _rmsnorm(mem, mem_norm, n_mem)
    wkv16 = w_mem_kv.astype(BF16)
    mk = _matmul(mh, wkv16[:, :hw], _epi_headnorm_mem,
                 (jnp.tile(mem_k_norm, MEM_HEADS).reshape(1, -1),), (_col_spec(1024),),
                 out_dtype=BF16, tm=n_mem, tn=1024, name="proj_mem_k")
    mv = _matmul(mh, wkv16[:, hw:], _epi_plain, out_dtype=BF16, tm=n_mem, tn=1024,
                 name="proj_mem_v")
    o_mem = _mem_attention(qm, mk, mv, min(512, s))

    merged = _merge(o_sb, o_dsa, o_mem, w_branch.astype(BF16), gates, min(512, s), 1024)
    x1, h2 = _out_proj(merged, w_out.astype(BF16), x, ffn_norm, 256)

    wu16 = w_up.astype(BF16)
    wg = _pad_cols(wu16[:, :D_FF], D_FF_PAD)
    wv = _pad_cols(wu16[:, D_FF:], D_FF_PAD)
    cwg = _pad_cols(conv_w[:, :D_FF], D_FF_PAD)
    cwv = _pad_cols(conv_w[:, D_FF:], D_FF_PAD)
    cbg = _pad_cols(conv_b[None, :D_FF], D_FF_PAD)
    cbv = _pad_cols(conv_b[None, D_FF:], D_FF_PAD)
    act = _ffn_up(h2, wg, wv, cwg, cwv, cbg, cbv, tm, 512)
    wd = jnp.pad(w_down.astype(BF16), ((0, D_FF_PAD - D_FF), (0, 0)))
    return _matmul(act, wd, _epi_residual, (x1,),
                   (pl.BlockSpec((tm, 1024), lambda i, j, k: (i, j)),),
                   out_dtype=F32, tm=tm, tn=1024, tk=D_FF_PAD // 4, name="ffn_down")


def kernel(x, mem, positions, attn_norm, mem_norm, w_in, b_gate, dsa_q_norm, dsa_k_norm,
           mem_q_norm, mem_k_norm, w_mem_kv, w_branch, w_out, ffn_norm, w_up, conv_w,
           conv_b, w_down):
    assert x.shape[0] == 1 and attn_norm.shape[0] == 1
    y = _layer(x[0], mem[0], positions[0], attn_norm[0], mem_norm[0], w_in[0], b_gate[0],
               dsa_q_norm[0], dsa_k_norm[0], mem_q_norm[0], mem_k_norm[0], w_mem_kv[0],
               w_branch[0], w_out[0], ffn_norm[0], w_up[0], conv_w[0], conv_b[0], w_down[0])
    return y[None]
```

```python
import functools

import numpy as np
import jax
import jax.numpy as jnp
from jax import lax
from jax.experimental import pallas as pl
from jax.experimental.pallas import tpu as pltpu

F32 = jnp.float32
BF16 = jnp.bfloat16
I32 = jnp.int32

D_MODEL = 2048
CHUNK = 64
CHUNK_SHIFT = 6
HEADS = 8
HEAD_DIM = 128
IDX_HEAD_DIM = 64
TOPK_MAX = 256
MEM_HEADS = 4
MEM_HEAD_DIM = 256
D_FF = 5504
D_FF_PAD = 5632
ROPE_THETA = 10000.0
NORM_EPS = 1e-6
LANES = 128
VMEM_LIMIT = 56 * 1024 * 1024

LOG2_E = 1.4426950408889634
NEG_BIG = -1e30
BIAS_MASKED = -2e30
KEY_TILE = 512
DSA_TQ = 512
DSA_TK = 1024
SB_TAIL_CUTOFF = 104.0


def _cparams(sem):
    return pltpu.CompilerParams(dimension_semantics=sem, vmem_limit_bytes=VMEM_LIMIT)


def _dot(a, b):
    return jnp.dot(a, b, preferred_element_type=F32)


def _dot_nt(a, b):
    return lax.dot_general(a, b, (((1,), (1,)), ((), ())), preferred_element_type=F32)


def _rmsnorm_body(x_ref, g_ref, o_ref):
    x = x_ref[...]
    ms = jnp.mean(x * x, axis=-1, keepdims=True)
    o_ref[...] = ((x * lax.rsqrt(ms + NORM_EPS)) * g_ref[...]).astype(o_ref.dtype)


def _rmsnorm(x, g, tm):
    m, d = x.shape
    return pl.pallas_call(
        _rmsnorm_body,
        out_shape=jax.ShapeDtypeStruct((m, d), BF16),
        grid=(m // tm,),
        in_specs=[pl.BlockSpec((tm, d), lambda i: (i, 0)),
                  pl.BlockSpec((1, d), lambda i: (0, 0))],
        out_specs=pl.BlockSpec((tm, d), lambda i: (i, 0)),
        compiler_params=_cparams(("parallel",)),
        name="rmsnorm",
    )(x, g.reshape(1, d))


def _epi_plain(y, o_ref):
    o_ref[...] = y.astype(o_ref.dtype)


def _head_rms(yh, g):
    ms = jnp.mean(yh * yh, axis=-1, keepdims=True)
    return (yh * lax.rsqrt(ms + NORM_EPS)) * g


def _epi_headnorm_rope(y, o_ref, g_ref, cos_ref, sin_ref):
    cos = cos_ref[...]
    sin = sin_ref[...]
    for h in range(y.shape[1] // HEAD_DIM):
        sl = slice(h * HEAD_DIM, (h + 1) * HEAD_DIM)
        yn = _head_rms(y[:, sl], g_ref[:, sl])
        o_ref[:, sl] = (yn * cos + pltpu.roll(yn, HEAD_DIM // 2, 1) * sin).astype(o_ref.dtype)


def _epi_headnorm_mem(y, o_ref, g_ref):
    for h in range(y.shape[1] // MEM_HEAD_DIM):
        sl = slice(h * MEM_HEAD_DIM, (h + 1) * MEM_HEAD_DIM)
        o_ref[:, sl] = _head_rms(y[:, sl], g_ref[:, sl]).astype(o_ref.dtype)


def _epi_rope_idx(y, o_ref, cos_ref, sin_ref):
    cos = cos_ref[...]
    sin = sin_ref[...]
    lane = lax.broadcasted_iota(I32, cos.shape, 1)
    first_half = (lane % IDX_HEAD_DIM) < (IDX_HEAD_DIM // 2)
    for t in range(y.shape[1] // LANES):
        sl = slice(t * LANES, (t + 1) * LANES)
        yt = y[:, sl]
        partner = jnp.where(first_half,
                            pltpu.roll(yt, LANES - IDX_HEAD_DIM // 2, 1),
                            pltpu.roll(yt, IDX_HEAD_DIM // 2, 1))
        o_ref[:, sl] = (yt * cos + partner * sin).astype(o_ref.dtype)


def _epi_scale(y, o_ref, *, scale):
    o_ref[...] = (y * scale).astype(o_ref.dtype)


def _epi_gate(y, o_ref, b_ref):
    o_ref[...] = jax.nn.sigmoid(y + b_ref[...]).astype(o_ref.dtype)


def _epi_residual(y, o_ref, r_ref):
    o_ref[...] = (r_ref[...] + y).astype(o_ref.dtype)


def _matmul_body(a_ref, w_ref, *rest, epi, n_extra, nk):
    extras = rest[:n_extra]
    o_ref = rest[n_extra]
    if nk == 1:
        epi(_dot(a_ref[...], w_ref[...]), o_ref, *extras)
        return
    acc_ref = rest[n_extra + 1]
    k = pl.program_id(2)

    @pl.when(k == 0)
    def _():
        acc_ref[...] = jnp.zeros_like(acc_ref)

    acc_ref[...] += _dot(a_ref[...], w_ref[...])

    @pl.when(k == nk - 1)
    def _():
        epi(acc_ref[...], o_ref, *extras)


def _matmul(a, w, epi, extras=(), extra_specs=(), *, out_dtype, tm, tn, tk=None, name):
    m, kdim = a.shape
    n = w.shape[1]
    tk = kdim if tk is None else tk
    nk = kdim // tk
    assert m % tm == 0 and n % tn == 0 and kdim % tk == 0
    scratch = [pltpu.VMEM((tm, tn), F32)] if nk > 1 else []
    return pl.pallas_call(
        functools.partial(_matmul_body, epi=epi, n_extra=len(extras), nk=nk),
        out_shape=jax.ShapeDtypeStruct((m, n), out_dtype),
        grid=(m // tm, n // tn, nk),
        in_specs=[pl.BlockSpec((tm, tk), lambda i, j, k: (i, k)),
                  pl.BlockSpec((tk, tn), lambda i, j, k: (k, j))] + list(extra_specs),
        out_specs=pl.BlockSpec((tm, tn), lambda i, j, k: (i, j)),
        scratch_shapes=scratch,
        compiler_params=_cparams(("parallel", "parallel", "arbitrary")),
        name=name,
    )(a, w, *extras)


def _row_spec(tm, width):
    return pl.BlockSpec((tm, width), lambda i, j, k: (i, 0))


def _col_spec(tn):
    return pl.BlockSpec((1, tn), lambda i, j, k: (0, j))


def _sb_body(q_ref, k_ref, v_ref, o_ref, acc_ref, r_ref, *, t, scale):
    i = pl.program_id(1)
    q = q_ref[...]
    row = lax.broadcasted_iota(I32, (t, t), 0)
    col = lax.broadcasted_iota(I32, (t, t), 1)
    upper = jnp.where(row > col, 1.0, 0.0).astype(BF16)
    visible = col < row

    def block(kb, diagonal):
        start = pl.multiple_of(kb * t, t)
        k = k_ref[pl.ds(start, t), :]
        v = v_ref[pl.ds(start, t), :]
        z = _dot_nt(q, k) * scale
        l1p = jnp.log1p(jnp.exp(-jnp.abs(z)))
        sp = jnp.maximum(z, 0.0) + l1p
        logsig = jnp.minimum(z, 0.0) - l1p
        if diagonal:
            sp = jnp.where(visible, sp, 0.0)
        hi = sp.astype(BF16)
        lo = (sp - hi.astype(F32)).astype(BF16)
        within = _dot(hi, upper) + _dot(lo, upper)
        r_old = r_ref[...]
        a = jnp.exp(logsig - (r_old + within))
        if diagonal:
            a = jnp.where(visible, a, 0.0)
        acc_ref[...] += _dot(a.astype(BF16), v)
        r_new = r_old + jnp.sum(sp, axis=-1, keepdims=True)
        r_ref[...] = r_new
        return jnp.min(r_new)

    acc_ref[...] = jnp.zeros_like(acc_ref)
    r_ref[...] = jnp.zeros_like(r_ref)
    r_min = block(i, True)

    def cond(c):
        kb, r_min = c
        return jnp.logical_and(kb >= 0, r_min < SB_TAIL_CUTOFF)

    def body(c):
        kb, _ = c
        return kb - 1, block(kb, False)

    lax.while_loop(cond, body, (i - 1, r_min))
    o_ref[...] = acc_ref[...].astype(o_ref.dtype)


def _stick_breaking(p, s, t):
    return pl.pallas_call(
        functools.partial(_sb_body, t=t, scale=HEAD_DIM ** -0.5),
        out_shape=jax.ShapeDtypeStruct((s, HEADS * HEAD_DIM), BF16),
        grid=(HEADS, s // t),
        in_specs=[pl.BlockSpec((t, HEAD_DIM), lambda h, i: (i, h)),
                  pl.BlockSpec((s, HEAD_DIM), lambda h, i: (0, HEADS + h)),
                  pl.BlockSpec((s, HEAD_DIM), lambda h, i: (0, 2 * HEADS + h))],
        out_specs=pl.BlockSpec((t, HEAD_DIM), lambda h, i: (i, h)),
        scratch_shapes=[pltpu.VMEM((t, HEAD_DIM), F32), pltpu.VMEM((t, 1), F32)],
        compiler_params=_cparams(("parallel", "parallel")),
        name="stick_breaking",
    )(p, p, p)


def _split_index_heads(qi_ref):
    lane = lax.broadcasted_iota(I32, (qi_ref.shape[0], LANES), 1)
    halves = []
    for pair in range(HEADS // 2):
        qp = qi_ref[:, pair * LANES:(pair + 1) * LANES]
        zero = jnp.zeros_like(qp)
        halves.append(jnp.where(lane < IDX_HEAD_DIM, qp, zero))
        halves.append(jnp.where(lane >= IDX_HEAD_DIM, qp, zero))
    return halves


def _index_scores(q_halves, kk, wcols):
    score = None
    for h in range(HEADS):
        term = jnp.maximum(_dot_nt(q_halves[h], kk), 0.0) * wcols[h]
        score = term if score is None else score + term
    return jnp.where(score == 0.0, 0.0, score)


def _admissible(q0, k0, shape):
    qpos = q0 + lax.broadcasted_iota(I32, shape, 0)
    kpos = k0 + lax.broadcasted_iota(I32, shape, 1)
    return (kpos >> CHUNK_SHIFT) <= (qpos >> CHUNK_SHIFT), kpos


def _ordered_key(x):
    b = lax.bitcast_convert_type(x, I32)
    return b ^ ((b >> 31) & jnp.int32(0x7FFFFFFF))


KEY_NEG_INF = int(np.array(-np.inf, np.float32).view(np.int32) ^ 0x7FFFFFFF)
ROW_GROUP = 64


def _key_to_float(k):
    return lax.bitcast_convert_type(k ^ ((k >> 31) & jnp.int32(0x7FFFFFFF)), F32)


def _any(mask):
    return jnp.max(jnp.where(mask, 1.0, 0.0)) > 0.5


def _dsa_select_body(qi_ref, kk_ref, w_ref, bias_ref, key_ref, mx_ref, mn_ref, st_ref,
                     *, t, topk, s_total):
    i = pl.program_id(0)
    n_tiles = (i * t + t + KEY_TILE - 1) // KEY_TILE
    n_all = s_total // KEY_TILE
    lane_tiles = KEY_TILE // LANES
    w = w_ref[...]
    wcols = [w[:, h:h + 1] for h in range(HEADS)]
    q_halves = _split_index_heads(qi_ref)
    (lo_ref, hi_ref, clo_ref, chi_ref, aux_ref, cnt_ref, pos_ref, nonneg_ref) = (
        st_ref.at[k] for k in range(8))
    shape = (t, LANES)
    mx_ref[...] = jnp.full(shape, -jnp.inf, F32)
    mn_ref[...] = jnp.full(shape, jnp.inf, F32)
    pos_ref[...] = jnp.zeros(shape, I32)
    nonneg_ref[...] = jnp.zeros(shape, I32)

    def fill(c, carry):
        start = pl.multiple_of(c * KEY_TILE, KEY_TILE)
        score = _index_scores(q_halves, kk_ref[pl.ds(start, KEY_TILE), :], wcols)
        adm, _ = _admissible(i * t, c * KEY_TILE, (t, KEY_TILE))
        low = jnp.where(adm, score, -jnp.inf)
        high = jnp.where(adm, score, jnp.inf)
        key_ref[c] = _ordered_key(low)
        mx = mx_ref[...]
        mn = mn_ref[...]
        pos = pos_ref[...]
        nonneg = nonneg_ref[...]
        for sub in range(lane_tiles):
            sl = slice(sub * LANES, (sub + 1) * LANES)
            mx = jnp.maximum(mx, low[:, sl])
            mn = jnp.minimum(mn, high[:, sl])
            pos = pos + jnp.where(low[:, sl] > 0.0, 1, 0).astype(I32)
            nonneg = nonneg + jnp.where(low[:, sl] >= 0.0, 1, 0).astype(I32)
        mx_ref[...] = mx
        mn_ref[...] = mn
        pos_ref[...] = pos
        nonneg_ref[...] = nonneg
        return carry

    lax.fori_loop(0, n_tiles, fill, 0)

    groups = [slice(g * ROW_GROUP, (g + 1) * ROW_GROUP) for g in range(t // ROW_GROUP)]
    col_in_tile = lax.broadcasted_iota(I32, (ROW_GROUP, LANES), 1)

    def lane_total(x):
        return jnp.broadcast_to(jnp.sum(x, axis=1, keepdims=True), shape)

    def count(pred):
        for rows in groups:
            lo_g = lo_ref[rows, :]
            aux_g = aux_ref[rows, :]

            def tile(c, cnt):
                keys = key_ref[c, rows, :]
                for sub in range(lane_tiles):
                    kcol = c * KEY_TILE + sub * LANES + col_in_tile
                    hit = pred(keys[:, sub * LANES:(sub + 1) * LANES], kcol, lo_g, aux_g)
                    cnt = cnt + jnp.where(hit, 1, 0).astype(I32)
                return cnt
            cnt_ref[rows, :] = lax.fori_loop(0, n_tiles, tile,
                                             jnp.zeros((ROW_GROUP, LANES), I32))
        return lane_total(cnt_ref[...])

    row_max = jnp.broadcast_to(jnp.max(mx_ref[...], axis=1, keepdims=True), shape)
    row_min = jnp.broadcast_to(jnp.min(mn_ref[...], axis=1, keepdims=True), shape)
    n_pos = lane_total(pos_ref[...])
    n_nonneg = lane_total(nonneg_ref[...])
    qpos = i * t + lax.broadcasted_iota(I32, shape, 0)
    n_adm = ((qpos >> CHUNK_SHIFT) + 1) * CHUNK
    enough = n_adm >= topk

    def active(lo, hi, c_lo):
        return jnp.logical_and(c_lo != topk, hi > lo + 1)

    positive = n_pos >= topk
    negative = n_nonneg < topk
    lo0 = jnp.where(positive, 1, jnp.where(negative, _ordered_key(row_min), 0))
    c_lo0 = jnp.where(positive, n_pos, jnp.where(negative, n_adm, n_nonneg))
    hi0 = jnp.where(positive, _ordered_key(row_max) + 1, jnp.where(negative, 0, 1))
    c_hi0 = jnp.where(positive, 0, jnp.where(negative, n_nonneg, n_pos))
    lo_ref[...] = jnp.where(enough, lo0, KEY_NEG_INF)
    hi_ref[...] = jnp.where(enough, hi0, KEY_NEG_INF + 1)
    clo_ref[...] = jnp.where(enough, c_lo0, topk)
    chi_ref[...] = jnp.where(enough, c_hi0, 0)

    def step(geometric):
        lo = lo_ref[...]
        hi = hi_ref[...]
        c_lo = clo_ref[...]
        act = active(lo, hi, c_lo)
        key_mid = (lo >> 1) + (hi >> 1) + (lo & hi & 1)
        if geometric:
            cand = key_mid
        else:
            mid = _ordered_key(0.5 * _key_to_float(lo) + 0.5 * _key_to_float(hi))
            cand = jnp.where(jnp.logical_and(mid > lo, mid < hi), mid, key_mid)
        aux_ref[...] = cand
        cnt = count(lambda keys, kcol, lo_g, cand_g: keys >= cand_g)
        up = jnp.logical_and(act, cnt >= topk)
        down = jnp.logical_and(act, cnt < topk)
        lo_ref[...] = jnp.where(up, cand, lo)
        clo_ref[...] = jnp.where(up, cnt, c_lo)
        hi_ref[...] = jnp.where(down, cand, hi)
        chi_ref[...] = jnp.where(down, cnt, chi_ref[...])

    def any_active():
        return _any(active(lo_ref[...], hi_ref[...], clo_ref[...])).astype(I32)

    def three_steps(_):
        step(False)
        step(False)
        step(True)
        return any_active()

    lax.while_loop(lambda go: go > 0, three_steps, any_active())

    need = topk - chi_ref[...]
    tie = clo_ref[...] > topk

    def tie_cut():
        def idx_step(b, p):
            cand = p | (jnp.int32(1) << (s_total.bit_length() - 1 - b))
            aux_ref[...] = cand
            cnt = count(lambda keys, kcol, thr_g, cand_g:
                        jnp.logical_and(keys == thr_g, kcol < cand_g))
            return jnp.where(cnt < need, cand, p)
        p = lax.fori_loop(0, s_total.bit_length(), idx_step, jnp.zeros(shape, I32))
        aux_ref[...] = jnp.where(tie, p, s_total)

    aux_ref[...] = jnp.full(shape, s_total, I32)
    pl.when(_any(tie))(tie_cut)

    for rows in groups:
        thr = lo_ref[rows, :]
        jcut = aux_ref[rows, :]

        def emit(c, carry):
            keys = key_ref[c, rows, :]
            parts = []
            for sub in range(lane_tiles):
                ksub = keys[:, sub * LANES:(sub + 1) * LANES]
                kcol = c * KEY_TILE + sub * LANES + col_in_tile
                chosen = jnp.logical_or(ksub > thr,
                                        jnp.logical_and(ksub == thr, kcol <= jcut))
                sel = jnp.logical_and(chosen, ksub > KEY_NEG_INF)
                parts.append(jnp.where(sel, 0.0, BIAS_MASKED))
            bias_ref[0, c, rows, :] = jnp.concatenate(parts, axis=1).astype(BF16)
            return carry

        lax.fori_loop(0, n_tiles, emit, 0)

    def pad(c, carry):
        bias_ref[0, c] = jnp.full((t, KEY_TILE), BIAS_MASKED, BF16)
        return carry

    lax.fori_loop(n_tiles, n_all, pad, 0)


def _dsa_select(qki, wi, s, t, topk):
    n_all = s // KEY_TILE
    return pl.pallas_call(
        functools.partial(_dsa_select_body, t=t, topk=topk, s_total=s),
        out_shape=jax.ShapeDtypeStruct((s // t, n_all, t, KEY_TILE), BF16),
        grid=(s // t,),
        in_specs=[pl.BlockSpec((t, 4 * LANES), lambda i: (i, 0)),
                  pl.BlockSpec((s, LANES), lambda i: (0, 4)),
                  pl.BlockSpec((t, LANES), lambda i: (i, 0))],
        out_specs=pl.BlockSpec((1, n_all, t, KEY_TILE), lambda i: (i, 0, 0, 0)),
        scratch_shapes=[pltpu.VMEM((n_all, t, KEY_TILE), I32),
                        pltpu.VMEM((t, LANES), F32), pltpu.VMEM((t, LANES), F32),
                        pltpu.VMEM((8, t, LANES), I32)],
        compiler_params=_cparams(("parallel",)),
        name="dsa_select",
    )(qki, qki, wi)


ATT_ROWS = 32


def _dsa_attn_body(qi_of, kj_of, last_of, q_ref, k_ref, v_ref, bias_ref, o_ref,
                   acc_ref, m_ref, l_ref, z_ref, p_ref, b_ref, *, scale):
    step = pl.program_id(0)
    tq, tk = z_ref.shape[1:]

    @pl.when(kj_of[step] == 0)
    def _():
        acc_ref[...] = jnp.zeros_like(acc_ref)
        m_ref[...] = jnp.full_like(m_ref, NEG_BIG)
        l_ref[...] = jnp.zeros_like(l_ref)

    c2 = scale * LOG2_E
    sel_t = bias_ref.shape[2]
    for a in range(bias_ref.shape[0]):
        for b in range(bias_ref.shape[1]):
            b_ref[a * sel_t:(a + 1) * sel_t, b * KEY_TILE:(b + 1) * KEY_TILE] = (
                bias_ref[a, b].astype(F32))

    ones = jnp.ones((tk, HEAD_DIM), BF16)
    chunks = [slice(c * ATT_ROWS, (c + 1) * ATT_ROWS) for c in range(tq // ATT_ROWS)]
    for h in range(HEADS):
        sl = slice(h * HEAD_DIM, (h + 1) * HEAD_DIM)
        buf = h % 2
        z_ref[buf] = _dot_nt(q_ref[:, sl], k_ref[:, sl])
        m_old = m_ref[h]
        m_new = jnp.concatenate(
            [jnp.maximum(m_old[rows],
                         jnp.max(z_ref[buf, rows, :] * c2 + b_ref[rows, :], axis=-1, keepdims=True))
             for rows in chunks], axis=0)
        for rows in chunks:
            s = z_ref[buf, rows, :] * c2 + b_ref[rows, :]
            p_ref[buf, rows, :] = jnp.exp2(s - m_new[rows]).astype(BF16)
        alpha = jnp.exp2(m_old - m_new)
        pv = _dot(p_ref[buf], jnp.concatenate([v_ref[:, sl], ones], axis=1))
        acc_ref[:, sl] = alpha * acc_ref[:, sl] + pv[:, :HEAD_DIM]
        l_ref[h] = alpha * l_ref[h] + pv[:, HEAD_DIM:]
        m_ref[h] = m_new

    @pl.when(last_of[step] == 1)
    def _():
        for h in range(HEADS):
            sl = slice(h * HEAD_DIM, (h + 1) * HEAD_DIM)
            o_ref[:, sl] = (acc_ref[:, sl] / l_ref[h]).astype(o_ref.dtype)


def _dsa_attention(qkb, p, bias, s, tq, tk):
    sel_t = bias.shape[2]
    qi_of, kj_of, last_of = [], [], []
    for q in range(s // tq):
        n_tiles = (q * tq + tq + tk - 1) // tk
        qi_of += [q] * n_tiles
        kj_of += list(range(n_tiles))
        last_of += [0] * (n_tiles - 1) + [1]
    width = HEADS * HEAD_DIM
    grid_spec = pltpu.PrefetchScalarGridSpec(
        num_scalar_prefetch=3,
        grid=(len(qi_of),),
        in_specs=[
            pl.BlockSpec((tq, width), lambda g, qo, ko, lo: (qo[g], 0)),
            pl.BlockSpec((tk, width), lambda g, qo, ko, lo: (ko[g], 1)),
            pl.BlockSpec((tk, width), lambda g, qo, ko, lo: (ko[g], 3)),
            pl.BlockSpec((tq // sel_t, tk // KEY_TILE, sel_t, KEY_TILE),
                         lambda g, qo, ko, lo: (qo[g], ko[g], 0, 0)),
        ],
        out_specs=pl.BlockSpec((tq, width), lambda g, qo, ko, lo: (qo[g], 0)),
        scratch_shapes=[pltpu.VMEM((tq, width), F32),
                        pltpu.VMEM((HEADS, tq, 1), F32),
                        pltpu.VMEM((HEADS, tq, HEAD_DIM), F32),
                        pltpu.VMEM((2, tq, tk), F32),
                        pltpu.VMEM((2, tq, tk), BF16),
                        pltpu.VMEM((tq, tk), F32)],
    )
    to_i32 = lambda v: jnp.asarray(np.asarray(v, np.int32))
    return pl.pallas_call(
        functools.partial(_dsa_attn_body, scale=HEAD_DIM ** -0.5),
        out_shape=jax.ShapeDtypeStruct((s, width), BF16),
        grid_spec=grid_spec,
        compiler_params=_cparams(("arbitrary",)),
        name="dsa_attention",
    )(to_i32(qi_of), to_i32(kj_of), to_i32(last_of), qkb, qkb, p, bias)


def _mem_attn_body(q_ref, mk_ref, mv_ref, o_ref, *, scale):
    for h in range(MEM_HEADS):
        sl = slice(h * MEM_HEAD_DIM, (h + 1) * MEM_HEAD_DIM)
        sc = _dot_nt(q_ref[:, sl], mk_ref[:, sl]) * scale
        e = jnp.exp(sc - jnp.max(sc, axis=-1, keepdims=True))
        p = e / jnp.sum(e, axis=-1, keepdims=True)
        o_ref[:, sl] = _dot(p.astype(BF16), mv_ref[:, sl]).astype(o_ref.dtype)


def _mem_attention(qm, mk, mv, tm):
    s, width = qm.shape
    n_mem = mk.shape[0]
    return pl.pallas_call(
        functools.partial(_mem_attn_body, scale=MEM_HEAD_DIM ** -0.5),
        out_shape=jax.ShapeDtypeStruct((s, width), BF16),
        grid=(s // tm,),
        in_specs=[pl.BlockSpec((tm, width), lambda i: (i, 0)),
                  pl.BlockSpec((n_mem, width), lambda i: (0, 0)),
                  pl.BlockSpec((n_mem, width), lambda i: (0, 0))],
        out_specs=pl.BlockSpec((tm, width), lambda i: (i, 0)),
        compiler_params=_cparams(("parallel",)),
        name="mem_attention",
    )(qm, mk, mv)


def _merge_body(o0_ref, o1_ref, o2_ref, w_ref, g0_ref, g1_ref, g2_ref, out_ref):
    merged = (g0_ref[...] * _dot(o0_ref[...], w_ref[0])
              + g1_ref[...] * _dot(o1_ref[...], w_ref[1])
              + g2_ref[...] * _dot(o2_ref[...], w_ref[2]))
    out_ref[...] = merged.astype(out_ref.dtype)


def _merge(o_sb, o_dsa, o_mem, w_branch, gates, tm, tn):
    s, bw = o_sb.shape
    d = w_branch.shape[2]
    nj = d // tn
    o_spec = pl.BlockSpec((tm, bw), lambda i, j: (i, 0))
    gate_specs = [pl.BlockSpec((tm, tn), functools.partial(lambda i, j, b: (i, b * nj + j), b=b))
                  for b in range(3)]
    return pl.pallas_call(
        _merge_body,
        out_shape=jax.ShapeDtypeStruct((s, d), BF16),
        grid=(s // tm, nj),
        in_specs=[o_spec, o_spec, o_spec,
                  pl.BlockSpec((3, bw, tn), lambda i, j: (0, 0, j))] + gate_specs,
        out_specs=pl.BlockSpec((tm, tn), lambda i, j: (i, j)),
        compiler_params=_cparams(("parallel", "parallel")),
        name="gated_merge",
    )(o_sb, o_dsa, o_mem, w_branch, gates, gates, gates)


def _out_proj_body(m_ref, w_ref, x_ref, g_ref, x1_ref, h2_ref):
    x1 = x_ref[...] + _dot(m_ref[...], w_ref[...])
    x1_ref[...] = x1
    ms = jnp.mean(x1 * x1, axis=-1, keepdims=True)
    h2_ref[...] = ((x1 * lax.rsqrt(ms + NORM_EPS)) * g_ref[...]).astype(h2_ref.dtype)


def _out_proj(merged, w_out, x, g, tm):
    s, d = x.shape
    row = pl.BlockSpec((tm, d), lambda i: (i, 0))
    return pl.pallas_call(
        _out_proj_body,
        out_shape=(jax.ShapeDtypeStruct((s, d), F32), jax.ShapeDtypeStruct((s, d), BF16)),
        grid=(s // tm,),
        in_specs=[row, pl.BlockSpec((d, d), lambda i: (0, 0)), row,
                  pl.BlockSpec((1, d), lambda i: (0, 0))],
        out_specs=(row, row),
        compiler_params=_cparams(("parallel",)),
        name="out_proj",
    )(merged, w_out, x, g.reshape(1, d))


HALO = 8


def _ffn_up_body(h_ref, wg_ref, wv_ref, cwg_ref, cwv_ref, cbg_ref, cbv_ref, o_ref,
                 ug_ref, uv_ref, *, tm):
    i = pl.program_id(1)

    @pl.when(i == 0)
    def _():
        ug_ref[0:HALO, :] = jnp.zeros((HALO, ug_ref.shape[1]), F32)
        uv_ref[0:HALO, :] = jnp.zeros((HALO, uv_ref.shape[1]), F32)

    h = h_ref[...]
    ug_ref[HALO:HALO + tm, :] = _dot(h, wg_ref[...])
    uv_ref[HALO:HALO + tm, :] = _dot(h, wv_ref[...])

    def conv(u_ref, cw_ref, cb_ref):
        out = cb_ref[...] + u_ref[HALO - 2:HALO - 2 + tm, :] * cw_ref[0:1, :]
        out = out + u_ref[HALO - 1:HALO - 1 + tm, :] * cw_ref[1:2, :]
        return out + u_ref[HALO:HALO + tm, :] * cw_ref[2:3, :]

    g = conv(ug_ref, cwg_ref, cbg_ref)
    val = conv(uv_ref, cwv_ref, cbv_ref)
    o_ref[...] = (jax.nn.silu(g) * val).astype(o_ref.dtype)
    ug_ref[0:HALO, :] = ug_ref[tm:tm + HALO, :]
    uv_ref[0:HALO, :] = uv_ref[tm:tm + HALO, :]


def _ffn_up(h2, wg, wv, cwg, cwv, cbg, cbv, tm, tn):
    s, d = h2.shape
    n = wg.shape[1]
    wspec = pl.BlockSpec((d, tn), lambda j, i: (0, j))
    cwspec = pl.BlockSpec((3, tn), lambda j, i: (0, j))
    cbspec = pl.BlockSpec((1, tn), lambda j, i: (0, j))
    return pl.pallas_call(
        functools.partial(_ffn_up_body, tm=tm),
        out_shape=jax.ShapeDtypeStruct((s, n), BF16),
        grid=(n // tn, s // tm),
        in_specs=[pl.BlockSpec((tm, d), lambda j, i: (i, 0)),
                  wspec, wspec, cwspec, cwspec, cbspec, cbspec],
        out_specs=pl.BlockSpec((tm, tn), lambda j, i: (i, j)),
        scratch_shapes=[pltpu.VMEM((tm + HALO, tn), F32), pltpu.VMEM((tm + HALO, tn), F32)],
        compiler_params=_cparams(("parallel", "arbitrary")),
        name="ffn_up_conv_gate",
    )(h2, wg, wv, cwg, cwv, cbg, cbv)


def _rope_tables(positions, head_dim):
    half = head_dim // 2
    inv = ROPE_THETA ** (-jnp.arange(half, dtype=F32) / half)
    ang = positions.astype(F32)[:, None] * inv
    cos = jnp.cos(ang)
    sin = jnp.sin(ang)
    reps = LANES // head_dim
    cos_t = jnp.tile(jnp.concatenate([cos, cos], axis=1), (1, reps))
    sin_t = jnp.tile(jnp.concatenate([-sin, sin], axis=1), (1, reps))
    return cos_t, sin_t


def _pad_cols(a, n):
    return jnp.pad(a, ((0, 0), (0, n - a.shape[1])))


def _layer(x, mem, positions, attn_norm, mem_norm, w_in, b_gate, dsa_q_norm, dsa_k_norm,
           mem_q_norm, mem_k_norm, w_mem_kv, w_branch, w_out, ffn_norm, w_up, conv_w,
           conv_b, w_down):
    s, d = x.shape
    assert d == D_MODEL and s % KEY_TILE == 0
    topk = min(TOPK_MAX, s // 4)
    tm = min(1024, s)
    ta = 256
    hw = HEADS * HEAD_DIM

    c_qa, c_ka, c_va, c_qb, c_kb, c_vb, c_qm = (k * hw for k in range(7))
    c_qi = 7 * hw
    c_ki = c_qi + HEADS * IDX_HEAD_DIM
    c_wi = c_ki + IDX_HEAD_DIM
    c_gl = c_wi + HEADS
    wb16 = w_in.astype(BF16)
    w_plain = jnp.concatenate([wb16[:, c_qa:c_qb], wb16[:, c_vb:c_qm]], axis=1)
    w_dsa_qk = wb16[:, c_qb:c_vb]
    w_qm = wb16[:, c_qm:c_qi]
    w_ki = wb16[:, c_ki:c_wi]
    w_idx = jnp.concatenate([wb16[:, c_qi:c_ki], w_ki, w_ki], axis=1)
    w_wi = _pad_cols(wb16[:, c_wi:c_gl], LANES)
    w_gate = wb16[:, c_gl:]

    cos_d, sin_d = _rope_tables(positions, HEAD_DIM)
    cos_i, sin_i = _rope_tables(positions, IDX_HEAD_DIM)

    h = _rmsnorm(x, attn_norm, min(512, s))

    p = _matmul(h, w_plain, _epi_plain, out_dtype=BF16, tm=tm, tn=1024, name="proj_plain")
    g_dsa = jnp.concatenate([jnp.tile(dsa_q_norm, HEADS), jnp.tile(dsa_k_norm, HEADS)])
    qkb = _matmul(h, w_dsa_qk, _epi_headnorm_rope,
                  (g_dsa.reshape(1, -1), cos_d, sin_d),
                  (_col_spec(1024), _row_spec(tm, LANES), _row_spec(tm, LANES)),
                  out_dtype=BF16, tm=tm, tn=1024, name="proj_dsa_qk")
    qm = _matmul(h, w_qm, _epi_headnorm_mem,
                 (jnp.tile(mem_q_norm, MEM_HEADS).reshape(1, -1),), (_col_spec(1024),),
                 out_dtype=BF16, tm=tm, tn=1024, name="proj_mem_q")
    qki = _matmul(h, w_idx, _epi_rope_idx, (cos_i, sin_i),
                  (_row_spec(tm, LANES), _row_spec(tm, LANES)),
                  out_dtype=BF16, tm=tm, tn=5 * LANES, name="proj_idx_qk")
    wi = _matmul(h, w_wi, functools.partial(_epi_scale, scale=HEADS ** -0.5 * IDX_HEAD_DIM ** -0.5),
                 out_dtype=F32, tm=tm, tn=LANES, name="proj_idx_w")
    gates = _matmul(h, w_gate, _epi_gate, (b_gate.reshape(1, -1),), (_col_spec(1024),),
                    out_dtype=F32, tm=tm, tn=1024, name="proj_gates")

    o_sb = _stick_breaking(p, s, ta)

    bias = _dsa_select(qki, wi, s, ta, topk)
    o_dsa = _dsa_attention(qkb, p, bias, s, min(DSA_TQ, s), min(DSA_TK, s))

    n_mem = mem.shape[0]
    mh = _rmsnorm(mem, mem_norm, n_mem)
    wkv16 = w_mem_kv.astype(BF16)
    mk = _matmul(mh, wkv16[:, :hw], _epi_headnorm_mem,
                 (jnp.tile(mem_k_norm, MEM_HEADS).reshape(1, -1),), (_col_spec(1024),),
                 out_dtype=BF16, tm=n_mem, tn=1024, name="proj_mem_k")
    mv = _matmul(mh, wkv16[:, hw:], _epi_plain, out_dtype=BF16, tm=n_mem, tn=1024,
                 name="proj_mem_v")
    o_mem = _mem_attention(qm, mk, mv, min(512, s))

    merged = _merge(o_sb, o_dsa, o_mem, w_branch.astype(BF16), gates, min(512, s), 1024)
    x1, h2 = _out_proj(merged, w_out.astype(BF16), x, ffn_norm, 256)

    wu16 = w_up.astype(BF16)
    wg = _pad_cols(wu16[:, :D_FF], D_FF_PAD)
    wv = _pad_cols(wu16[:, D_FF:], D_FF_PAD)
    cwg = _pad_cols(conv_w[:, :D_FF], D_FF_PAD)
    cwv = _pad_cols(conv_w[:, D_FF:], D_FF_PAD)
    cbg = _pad_cols(conv_b[None, :D_FF], D_FF_PAD)
    cbv = _pad_cols(conv_b[None, D_FF:], D_FF_PAD)
    act = _ffn_up(h2, wg, wv, cwg, cwv, cbg, cbv, tm, 512)
    wd = jnp.pad(w_down.astype(BF16), ((0, D_FF_PAD - D_FF), (0, 0)))
    return _matmul(act, wd, _epi_residual, (x1,),
                   (pl.BlockSpec((tm, 1024), lambda i, j, k: (i, j)),),
                   out_dtype=F32, tm=tm, tn=1024, tk=D_FF_PAD // 4, name="ffn_down")


def kernel(x, mem, positions, attn_norm, mem_norm, w_in, b_gate, dsa_q_norm, dsa_k_norm,
           mem_q_norm, mem_k_norm, w_mem_kv, w_branch, w_out, ffn_norm, w_up, conv_w,
           conv_b, w_down):
    assert x.shape[0] == 1 and attn_norm.shape[0] == 1
    y = _layer(x[0], mem[0], positions[0], attn_norm[0], mem_norm[0], w_in[0], b_gate[0],
               dsa_q_norm[0], dsa_k_norm[0], mem_q_norm[0], mem_k_norm[0], w_mem_kv[0],
               w_branch[0], w_out[0], ffn_norm[0], w_up[0], conv_w[0], conv_b[0], w_down[0])
    return y[None]
```

```python
import functools

import numpy as np
import jax
import jax.numpy as jnp
from jax import lax
from jax.experimental import pallas as pl
from jax.experimental.pallas import tpu as pltpu

F32 = jnp.float32
BF16 = jnp.bfloat16
I32 = jnp.int32

D_MODEL = 2048
CHUNK = 64
CHUNK_SHIFT = 6
HEADS = 8
HEAD_DIM = 128
IDX_HEAD_DIM = 64
TOPK_MAX = 256
MEM_HEADS = 4
MEM_HEAD_DIM = 256
D_FF = 5504
D_FF_PAD = 5632
ROPE_THETA = 10000.0
NORM_EPS = 1e-6
LANES = 128
VMEM_LIMIT = 56 * 1024 * 1024

LOG2_E = 1.4426950408889634
NEG_BIG = -1e30
BIAS_MASKED = -2e30
KEY_TILE = 512
DSA_TQ = 512
DSA_TK = 1024
SB_TAIL_CUTOFF = 104.0


def _cparams(sem):
    return pltpu.CompilerParams(dimension_semantics=sem, vmem_limit_bytes=VMEM_LIMIT)


def _dot(a, b):
    return jnp.dot(a, b, preferred_element_type=F32)


def _dot_nt(a, b):
    return lax.dot_general(a, b, (((1,), (1,)), ((), ())), preferred_element_type=F32)


def _rmsnorm_body(x_ref, g_ref, o_ref):
    x = x_ref[...]
    ms = jnp.mean(x * x, axis=-1, keepdims=True)
    o_ref[...] = ((x * lax.rsqrt(ms + NORM_EPS)) * g_ref[...]).astype(o_ref.dtype)


def _rmsnorm(x, g, tm):
    m, d = x.shape
    return pl.pallas_call(
        _rmsnorm_body,
        out_shape=jax.ShapeDtypeStruct((m, d), BF16),
        grid=(m // tm,),
        in_specs=[pl.BlockSpec((tm, d), lambda i: (i, 0)),
                  pl.BlockSpec((1, d), lambda i: (0, 0))],
        out_specs=pl.BlockSpec((tm, d), lambda i: (i, 0)),
        compiler_params=_cparams(("parallel",)),
        name="rmsnorm",
    )(x, g.reshape(1, d))


def _epi_plain(y, o_ref):
    o_ref[...] = y.astype(o_ref.dtype)


def _epi_headnorm_rope(y, o_ref, g_ref, cos_ref, sin_ref):
    cos = cos_ref[...]
    sin = sin_ref[...]
    heads = [slice(h * HEAD_DIM, (h + 1) * HEAD_DIM) for h in range(y.shape[1] // HEAD_DIM)]
    ms = [jnp.mean(y[:, sl] * y[:, sl], axis=-1, keepdims=True) for sl in heads]
    yn = [(y[:, sl] * lax.rsqrt(m + NORM_EPS)) * g_ref[:, sl] for sl, m in zip(heads, ms)]
    rot = [pltpu.roll(v, HEAD_DIM // 2, 1) for v in yn]
    for sl, v, r in zip(heads, yn, rot):
        o_ref[:, sl] = (v * cos + r * sin).astype(o_ref.dtype)


def _epi_headnorm_mem(y, o_ref, g_ref):
    heads = [slice(h * MEM_HEAD_DIM, (h + 1) * MEM_HEAD_DIM)
             for h in range(y.shape[1] // MEM_HEAD_DIM)]
    ms = [jnp.mean(y[:, sl] * y[:, sl], axis=-1, keepdims=True) for sl in heads]
    for sl, m in zip(heads, ms):
        o_ref[:, sl] = ((y[:, sl] * lax.rsqrt(m + NORM_EPS)) * g_ref[:, sl]).astype(o_ref.dtype)


def _epi_rope_idx(y, o_ref, cos_ref, sin_ref):
    cos = cos_ref[...]
    sin = sin_ref[...]
    lane = lax.broadcasted_iota(I32, cos.shape, 1)
    first_half = (lane % IDX_HEAD_DIM) < (IDX_HEAD_DIM // 2)
    tiles = [slice(t * LANES, (t + 1) * LANES) for t in range(y.shape[1] // LANES)]
    up = [pltpu.roll(y[:, sl], LANES - IDX_HEAD_DIM // 2, 1) for sl in tiles]
    down = [pltpu.roll(y[:, sl], IDX_HEAD_DIM // 2, 1) for sl in tiles]
    for sl, u, d in zip(tiles, up, down):
        o_ref[:, sl] = (y[:, sl] * cos + jnp.where(first_half, u, d) * sin).astype(o_ref.dtype)


def _epi_scale(y, o_ref, *, scale):
    o_ref[...] = (y * scale).astype(o_ref.dtype)


def _epi_gate(y, o_ref, b_ref):
    o_ref[...] = jax.nn.sigmoid(y + b_ref[...]).astype(o_ref.dtype)


def _epi_residual(y, o_ref, r_ref):
    o_ref[...] = (r_ref[...] + y).astype(o_ref.dtype)


def _matmul_body(a_ref, w_ref, *rest, epi, n_extra, nk):
    extras = rest[:n_extra]
    o_ref = rest[n_extra]
    if nk == 1:
        epi(_dot(a_ref[...], w_ref[...]), o_ref, *extras)
        return
    acc_ref = rest[n_extra + 1]
    k = pl.program_id(2)

    @pl.when(k == 0)
    def _():
        acc_ref[...] = jnp.zeros_like(acc_ref)

    acc_ref[...] += _dot(a_ref[...], w_ref[...])

    @pl.when(k == nk - 1)
    def _():
        epi(acc_ref[...], o_ref, *extras)


def _matmul(a, w, epi, extras=(), extra_specs=(), *, out_dtype, tm, tn, tk=None, name):
    m, kdim = a.shape
    n = w.shape[1]
    tk = kdim if tk is None else tk
    nk = kdim // tk
    assert m % tm == 0 and n % tn == 0 and kdim % tk == 0
    scratch = [pltpu.VMEM((tm, tn), F32)] if nk > 1 else []
    return pl.pallas_call(
        functools.partial(_matmul_body, epi=epi, n_extra=len(extras), nk=nk),
        out_shape=jax.ShapeDtypeStruct((m, n), out_dtype),
        grid=(m // tm, n // tn, nk),
        in_specs=[pl.BlockSpec((tm, tk), lambda i, j, k: (i, k)),
                  pl.BlockSpec((tk, tn), lambda i, j, k: (k, j))] + list(extra_specs),
        out_specs=pl.BlockSpec((tm, tn), lambda i, j, k: (i, j)),
        scratch_shapes=scratch,
        compiler_params=_cparams(("parallel", "parallel", "arbitrary")),
        name=name,
    )(a, w, *extras)


def _row_spec(tm, width):
    return pl.BlockSpec((tm, width), lambda i, j, k: (i, 0))


def _col_spec(tn):
    return pl.BlockSpec((1, tn), lambda i, j, k: (0, j))


SB_GROUP = 2


def _sb_body(q_ref, k_ref, v_ref, o_ref, acc_ref, r_ref, *, t, scale):
    i = pl.program_id(1)
    row = lax.broadcasted_iota(I32, (t, t), 0)
    col = lax.broadcasted_iota(I32, (t, t), 1)
    upper = jnp.where(row > col, 1.0, 0.0).astype(BF16)
    visible = col < row

    def block(kb, diagonal):
        start = pl.multiple_of(kb * t, t)
        r_min = None
        for g in range(SB_GROUP):
            sl = slice(g * HEAD_DIM, (g + 1) * HEAD_DIM)
            k = k_ref[pl.ds(start, t), sl]
            v = v_ref[pl.ds(start, t), sl]
            z = _dot_nt(q_ref[:, sl], k) * scale
            l1p = jnp.log1p(jnp.exp(-jnp.abs(z)))
            sp = jnp.maximum(z, 0.0) + l1p
            logsig = jnp.minimum(z, 0.0) - l1p
            if diagonal:
                sp = jnp.where(visible, sp, 0.0)
            hi = sp.astype(BF16)
            lo = (sp - hi.astype(F32)).astype(BF16)
            within = _dot(hi, upper) + _dot(lo, upper)
            r_old = r_ref[g]
            a = jnp.exp(logsig - (r_old + within))
            if diagonal:
                a = jnp.where(visible, a, 0.0)
            acc_ref[:, sl] += _dot(a.astype(BF16), v)
            r_new = r_old + jnp.sum(sp, axis=-1, keepdims=True)
            r_ref[g] = r_new
            r_head = jnp.min(r_new)
            r_min = r_head if r_min is None else jnp.minimum(r_min, r_head)
        return r_min

    acc_ref[...] = jnp.zeros_like(acc_ref)
    r_ref[...] = jnp.zeros_like(r_ref)
    r_min = block(i, True)

    def cond(c):
        kb, r_min = c
        return jnp.logical_and(kb >= 0, r_min < SB_TAIL_CUTOFF)

    def body(c):
        kb, _ = c
        return kb - 1, block(kb, False)

    lax.while_loop(cond, body, (i - 1, r_min))
    o_ref[...] = acc_ref[...].astype(o_ref.dtype)


def _stick_breaking(p, s, t):
    groups = HEADS // SB_GROUP
    width = SB_GROUP * HEAD_DIM
    return pl.pallas_call(
        functools.partial(_sb_body, t=t, scale=HEAD_DIM ** -0.5),
        out_shape=jax.ShapeDtypeStruct((s, HEADS * HEAD_DIM), BF16),
        grid=(groups, s // t),
        in_specs=[pl.BlockSpec((t, width), lambda h, i: (i, h)),
                  pl.BlockSpec((s, width), lambda h, i: (0, groups + h)),
                  pl.BlockSpec((s, width), lambda h, i: (0, 2 * groups + h))],
        out_specs=pl.BlockSpec((t, width), lambda h, i: (i, h)),
        scratch_shapes=[pltpu.VMEM((t, width), F32), pltpu.VMEM((SB_GROUP, t, 1), F32)],
        compiler_params=_cparams(("parallel", "parallel")),
        name="stick_breaking",
    )(p, p, p)


def _split_index_heads(qi_ref):
    lane = lax.broadcasted_iota(I32, (qi_ref.shape[0], LANES), 1)
    halves = []
    for pair in range(HEADS // 2):
        qp = qi_ref[:, pair * LANES:(pair + 1) * LANES]
        zero = jnp.zeros_like(qp)
        halves.append(jnp.where(lane < IDX_HEAD_DIM, qp, zero))
        halves.append(jnp.where(lane >= IDX_HEAD_DIM, qp, zero))
    return halves


def _index_scores(q_halves, kk, wcols):
    score = None
    for h in range(HEADS):
        term = jnp.maximum(_dot_nt(q_halves[h], kk), 0.0) * wcols[h]
        score = term if score is None else score + term
    return jnp.where(score == 0.0, 0.0, score)


def _admissible(q0, k0, shape):
    qpos = q0 + lax.broadcasted_iota(I32, shape, 0)
    kpos = k0 + lax.broadcasted_iota(I32, shape, 1)
    return (kpos >> CHUNK_SHIFT) <= (qpos >> CHUNK_SHIFT), kpos


def _ordered_key(x):
    b = lax.bitcast_convert_type(x, I32)
    return b ^ ((b >> 31) & jnp.int32(0x7FFFFFFF))


KEY_NEG_INF = int(np.array(-np.inf, np.float32).view(np.int32) ^ 0x7FFFFFFF)
ROW_GROUP = 128


def _key_to_float(k):
    return lax.bitcast_convert_type(k ^ ((k >> 31) & jnp.int32(0x7FFFFFFF)), F32)


def _any(mask):
    return jnp.max(jnp.where(mask, 1.0, 0.0)) > 0.5


def _dsa_select_body(qi_ref, kk_ref, w_ref, bias_ref, key_ref, mx_ref, mn_ref, st_ref,
                     *, t, topk, s_total):
    i = pl.program_id(0)
    n_tiles = (i * t + t + KEY_TILE - 1) // KEY_TILE
    n_all = s_total // KEY_TILE
    lane_tiles = KEY_TILE // LANES
    w = w_ref[...]
    wcols = [w[:, h:h + 1] for h in range(HEADS)]
    q_halves = _split_index_heads(qi_ref)
    (lo_ref, hi_ref, clo_ref, chi_ref, aux_ref, cnt_ref, pos_ref, nonneg_ref) = (
        st_ref.at[k] for k in range(8))
    shape = (t, LANES)
    mx_ref[...] = jnp.full(shape, -jnp.inf, F32)
    mn_ref[...] = jnp.full(shape, jnp.inf, F32)
    pos_ref[...] = jnp.zeros(shape, I32)
    nonneg_ref[...] = jnp.zeros(shape, I32)

    def fill(c, carry, *, masked):
        start = pl.multiple_of(c * KEY_TILE, KEY_TILE)
        score = _index_scores(q_halves, kk_ref[pl.ds(start, KEY_TILE), :], wcols)
        if masked:
            adm, _ = _admissible(i * t, c * KEY_TILE, (t, KEY_TILE))
            low = jnp.where(adm, score, -jnp.inf)
            high = jnp.where(adm, score, jnp.inf)
        else:
            low = high = score
        key_ref[c] = _ordered_key(low)
        mx = mx_ref[...]
        mn = mn_ref[...]
        pos = pos_ref[...]
        nonneg = nonneg_ref[...]
        for sub in range(lane_tiles):
            sl = slice(sub * LANES, (sub + 1) * LANES)
            mx = jnp.maximum(mx, low[:, sl])
            mn = jnp.minimum(mn, high[:, sl])
            pos = pos + jnp.where(low[:, sl] > 0.0, 1, 0).astype(I32)
            nonneg = nonneg + jnp.where(low[:, sl] >= 0.0, 1, 0).astype(I32)
        mx_ref[...] = mx
        mn_ref[...] = mn
        pos_ref[...] = pos
        nonneg_ref[...] = nonneg
        return carry

    n_full = (i * (t // CHUNK) + 1) // (KEY_TILE // CHUNK)
    lax.fori_loop(0, n_full, functools.partial(fill, masked=False), 0)
    lax.fori_loop(n_full, n_tiles, functools.partial(fill, masked=True), 0)

    groups =[slice(g * ROW_GROUP, (g + 1) * ROW_GROUP) for g in range(t // ROW_GROUP)]
    col_in_tile = lax.broadcasted_iota(I32, (ROW_GROUP, LANES), 1)

    def lane_total(x):
        return jnp.broadcast_to(jnp.sum(x, axis=1, keepdims=True), shape)

    def count(pred):
        for rows in groups:
            lo_g = lo_ref[rows, :]
            aux_g = aux_ref[rows, :]

            def tile(c, cnt):
                keys = key_ref[c, rows, :]
                for sub in range(lane_tiles):
                    kcol = c * KEY_TILE + sub * LANES + col_in_tile
                    hit = pred(keys[:, sub * LANES:(sub + 1) * LANES], kcol, lo_g, aux_g)
                    cnt = cnt + jnp.where(hit, 1, 0).astype(I32)
                return cnt
            cnt_ref[rows, :] = lax.fori_loop(0, n_tiles, tile,
                                             jnp.zeros((ROW_GROUP, LANES), I32))
        return lane_total(cnt_ref[...])

    row_max = jnp.broadcast_to(jnp.max(mx_ref[...], axis=1, keepdims=True), shape)
    row_min = jnp.broadcast_to(jnp.min(mn_ref[...], axis=1, keepdims=True), shape)
    n_pos = lane_total(pos_ref[...])
    n_nonneg = lane_total(nonneg_ref[...])
    qpos = i * t + lax.broadcasted_iota(I32, shape, 0)
    n_adm = ((qpos >> CHUNK_SHIFT) + 1) * CHUNK
    enough = n_adm >= topk

    def active(lo, hi, c_lo):
        return jnp.logical_and(c_lo != topk, hi > lo + 1)

    positive = n_pos >= topk
    negative = n_nonneg < topk
    lo0 = jnp.where(positive, 1, jnp.where(negative, _ordered_key(row_min), 0))
    c_lo0 = jnp.where(positive, n_pos, jnp.where(negative, n_adm, n_nonneg))
    hi0 = jnp.where(positive, _ordered_key(row_max) + 1, jnp.where(negative, 0, 1))
    c_hi0 = jnp.where(positive, 0, jnp.where(negative, n_nonneg, n_pos))
    lo_ref[...] = jnp.where(enough, lo0, KEY_NEG_INF)
    hi_ref[...] = jnp.where(enough, hi0, KEY_NEG_INF + 1)
    clo_ref[...] = jnp.where(enough, c_lo0, topk)
    chi_ref[...] = jnp.where(enough, c_hi0, 0)

    def step(geometric):
        lo = lo_ref[...]
        hi = hi_ref[...]
        c_lo = clo_ref[...]
        act = active(lo, hi, c_lo)
        key_mid = (lo >> 1) + (hi >> 1) + (lo & hi & 1)
        if geometric:
            cand = key_mid
        else:
            mid = _ordered_key(0.5 * _key_to_float(lo) + 0.5 * _key_to_float(hi))
            cand = jnp.where(jnp.logical_and(mid > lo, mid < hi), mid, key_mid)
        aux_ref[...] = cand
        cnt = count(lambda keys, kcol, lo_g, cand_g: keys >= cand_g)
        up = jnp.logical_and(act, cnt >= topk)
        down = jnp.logical_and(act, cnt < topk)
        lo_ref[...] = jnp.where(up, cand, lo)
        clo_ref[...] = jnp.where(up, cnt, c_lo)
        hi_ref[...] = jnp.where(down, cand, hi)
        chi_ref[...] = jnp.where(down, cnt, chi_ref[...])

    def any_active():
        return _any(active(lo_ref[...], hi_ref[...], clo_ref[...])).astype(I32)

    def three_steps(_):
        step(False)
        step(False)
        step(True)
        return any_active()

    lax.while_loop(lambda go: go > 0, three_steps, any_active())

    need = topk - chi_ref[...]
    tie = clo_ref[...] > topk

    def tie_cut():
        def idx_step(b, p):
            cand = p | (jnp.int32(1) << (s_total.bit_length() - 1 - b))
            aux_ref[...] = cand
            cnt = count(lambda keys, kcol, thr_g, cand_g:
                        jnp.logical_and(keys == thr_g, kcol < cand_g))
            return jnp.where(cnt < need, cand, p)
        p = lax.fori_loop(0, s_total.bit_length(), idx_step, jnp.zeros(shape, I32))
        aux_ref[...] = jnp.where(tie, p, s_total)

    aux_ref[...] = jnp.full(shape, s_total, I32)
    pl.when(_any(tie))(tie_cut)

    for rows in groups:
        thr = lo_ref[rows, :]
        jcut = aux_ref[rows, :]

        def emit(c, carry):
            keys = key_ref[c, rows, :]
            parts = []
            for sub in range(lane_tiles):
                ksub = keys[:, sub * LANES:(sub + 1) * LANES]
                kcol = c * KEY_TILE + sub * LANES + col_in_tile
                chosen = jnp.logical_or(ksub > thr,
                                        jnp.logical_and(ksub == thr, kcol <= jcut))
                sel = jnp.logical_and(chosen, ksub > KEY_NEG_INF)
                parts.append(jnp.where(sel, 0.0, BIAS_MASKED))
            bias_ref[0, c, rows, :] = jnp.concatenate(parts, axis=1).astype(BF16)
            return carry

        lax.fori_loop(0, n_tiles, emit, 0)

    def pad(c, carry):
        bias_ref[0, c] = jnp.full((t, KEY_TILE), BIAS_MASKED, BF16)
        return carry

    lax.fori_loop(n_tiles, n_all, pad, 0)


def _dsa_select(qki, wi, s, t, topk):
    n_all = s // KEY_TILE
    return pl.pallas_call(
        functools.partial(_dsa_select_body, t=t, topk=topk, s_total=s),
        out_shape=jax.ShapeDtypeStruct((s // t, n_all, t, KEY_TILE), BF16),
        grid=(s // t,),
        in_specs=[pl.BlockSpec((t, 4 * LANES), lambda i: (i, 0)),
                  pl.BlockSpec((s, LANES), lambda i: (0, 4)),
                  pl.BlockSpec((t, LANES), lambda i: (i, 0))],
        out_specs=pl.BlockSpec((1, n_all, t, KEY_TILE), lambda i: (i, 0, 0, 0)),
        scratch_shapes=[pltpu.VMEM((n_all, t, KEY_TILE), I32),
                        pltpu.VMEM((t, LANES), F32), pltpu.VMEM((t, LANES), F32),
                        pltpu.VMEM((8, t, LANES), I32)],
        compiler_params=_cparams(("parallel",)),
        name="dsa_select",
    )(qki, qki, wi)


ATT_ROWS = 64


def _dsa_attn_body(qi_of, kj_of, last_of, q_ref, k_ref, v_ref, bias_ref, o_ref,
                   acc_ref, m_ref, l_ref, z_ref, p_ref, b_ref, *, scale):
    step = pl.program_id(0)
    tq, tk = z_ref.shape[1:]

    @pl.when(kj_of[step] == 0)
    def _():
        acc_ref[...] = jnp.zeros_like(acc_ref)
        m_ref[...] = jnp.full_like(m_ref, NEG_BIG)
        l_ref[...] = jnp.zeros_like(l_ref)

    c2 = scale * LOG2_E
    sel_t = bias_ref.shape[2]
    for a in range(bias_ref.shape[0]):
        for b in range(bias_ref.shape[1]):
            b_ref[a * sel_t:(a + 1) * sel_t, b * KEY_TILE:(b + 1) * KEY_TILE] = (
                bias_ref[a, b].astype(F32))

    ones = jnp.ones((tk, HEAD_DIM), BF16)
    chunks = [slice(c * ATT_ROWS, (c + 1) * ATT_ROWS) for c in range(tq // ATT_ROWS)]
    for h in range(HEADS):
        sl = slice(h * HEAD_DIM, (h + 1) * HEAD_DIM)
        buf = h % 2
        s = _dot_nt(q_ref[:, sl], k_ref[:, sl]) * c2 + b_ref[...]
        z_ref[buf] = s
        m_old = m_ref[h]
        m_new = jnp.maximum(m_old, jnp.max(s, axis=-1, keepdims=True))
        for rows in chunks:
            p_ref[buf, rows, :] = jnp.exp2(z_ref[buf, rows, :] - m_new[rows]).astype(BF16)
        alpha = jnp.exp2(m_old - m_new)
        pv = _dot(p_ref[buf], jnp.concatenate([v_ref[:, sl], ones], axis=1))
        acc_ref[:, sl] = alpha * acc_ref[:, sl] + pv[:, :HEAD_DIM]
        l_ref[h] = alpha * l_ref[h] + pv[:, HEAD_DIM:]
        m_ref[h] = m_new

    @pl.when(last_of[step] == 1)
    def _():
        for h in range(HEADS):
            sl = slice(h * HEAD_DIM, (h + 1) * HEAD_DIM)
            o_ref[:, sl] = (acc_ref[:, sl] / l_ref[h]).astype(o_ref.dtype)


def _dsa_attention(qkb, p, bias, s, tq, tk):
    sel_t = bias.shape[2]
    qi_of, kj_of, last_of = [], [], []
    for q in range(s // tq):
        n_tiles = (q * tq + tq + tk - 1) // tk
        qi_of += [q] * n_tiles
        kj_of += list(range(n_tiles))
        last_of += [0] * (n_tiles - 1) + [1]
    width = HEADS * HEAD_DIM
    grid_spec = pltpu.PrefetchScalarGridSpec(
        num_scalar_prefetch=3,
        grid=(len(qi_of),),
        in_specs=[
            pl.BlockSpec((tq, width), lambda g, qo, ko, lo: (qo[g], 0)),
            pl.BlockSpec((tk, width), lambda g, qo, ko, lo: (ko[g], 1)),
            pl.BlockSpec((tk, width), lambda g, qo, ko, lo: (ko[g], 3)),
            pl.BlockSpec((tq // sel_t, tk // KEY_TILE, sel_t, KEY_TILE),
                         lambda g, qo, ko, lo: (qo[g], ko[g], 0, 0)),
        ],
        out_specs=pl.BlockSpec((tq, width), lambda g, qo, ko, lo: (qo[g], 0)),
        scratch_shapes=[pltpu.VMEM((tq, width), F32),
                        pltpu.VMEM((HEADS, tq, 1), F32),
                        pltpu.VMEM((HEADS, tq, HEAD_DIM), F32),
                        pltpu.VMEM((2, tq, tk), F32),
                        pltpu.VMEM((2, tq, tk), BF16),
                        pltpu.VMEM((tq, tk), F32)],
    )
    to_i32 = lambda v: jnp.asarray(np.asarray(v, np.int32))
    return pl.pallas_call(
        functools.partial(_dsa_attn_body, scale=HEAD_DIM ** -0.5),
        out_shape=jax.ShapeDtypeStruct((s, width), BF16),
        grid_spec=grid_spec,
        compiler_params=_cparams(("arbitrary",)),
        name="dsa_attention",
    )(to_i32(qi_of), to_i32(kj_of), to_i32(last_of), qkb, qkb, p, bias)


def _mem_attn_body(q_ref, mk_ref, mv_ref, o_ref, *, scale):
    for h in range(MEM_HEADS):
        sl = slice(h * MEM_HEAD_DIM, (h + 1) * MEM_HEAD_DIM)
        sc = _dot_nt(q_ref[:, sl], mk_ref[:, sl]) * scale
        e = jnp.exp(sc - jnp.max(sc, axis=-1, keepdims=True))
        p = e / jnp.sum(e, axis=-1, keepdims=True)
        o_ref[:, sl] = _dot(p.astype(BF16), mv_ref[:, sl]).astype(o_ref.dtype)


def _mem_attention(qm, mk, mv, tm):
    s, width = qm.shape
    n_mem = mk.shape[0]
    return pl.pallas_call(
        functools.partial(_mem_attn_body, scale=MEM_HEAD_DIM ** -0.5),
        out_shape=jax.ShapeDtypeStruct((s, width), BF16),
        grid=(s // tm,),
        in_specs=[pl.BlockSpec((tm, width), lambda i: (i, 0)),
                  pl.BlockSpec((n_mem, width), lambda i: (0, 0)),
                  pl.BlockSpec((n_mem, width), lambda i: (0, 0))],
        out_specs=pl.BlockSpec((tm, width), lambda i: (i, 0)),
        compiler_params=_cparams(("parallel",)),
        name="mem_attention",
    )(qm, mk, mv)


def _merge_body(o0_ref, o1_ref, o2_ref, w_ref, g0_ref, g1_ref, g2_ref, out_ref):
    merged = (g0_ref[...] * _dot(o0_ref[...], w_ref[0])
              + g1_ref[...] * _dot(o1_ref[...], w_ref[1])
              + g2_ref[...] * _dot(o2_ref[...], w_ref[2]))
    out_ref[...] = merged.astype(out_ref.dtype)


def _merge(o_sb, o_dsa, o_mem, w_branch, gates, tm, tn):
    s, bw = o_sb.shape
    d = w_branch.shape[2]
    nj = d // tn
    o_spec = pl.BlockSpec((tm, bw), lambda i, j: (i, 0))
    gate_specs = [pl.BlockSpec((tm, tn), functools.partial(lambda i, j, b: (i, b * nj + j), b=b))
                  for b in range(3)]
    return pl.pallas_call(
        _merge_body,
        out_shape=jax.ShapeDtypeStruct((s, d), BF16),
        grid=(s // tm, nj),
        in_specs=[o_spec, o_spec, o_spec,
                  pl.BlockSpec((3, bw, tn), lambda i, j: (0, 0, j))] + gate_specs,
        out_specs=pl.BlockSpec((tm, tn), lambda i, j: (i, j)),
        compiler_params=_cparams(("parallel", "parallel")),
        name="gated_merge",
    )(o_sb, o_dsa, o_mem, w_branch, gates, gates, gates)


def _out_proj_body(m_ref, w_ref, x_ref, g_ref, x1_ref, h2_ref):
    x1 = x_ref[...] + _dot(m_ref[...], w_ref[...])
    x1_ref[...] = x1
    ms = jnp.mean(x1 * x1, axis=-1, keepdims=True)
    h2_ref[...] = ((x1 * lax.rsqrt(ms + NORM_EPS)) * g_ref[...]).astype(h2_ref.dtype)


def _out_proj(merged, w_out, x, g, tm):
    s, d = x.shape
    row = pl.BlockSpec((tm, d), lambda i: (i, 0))
    return pl.pallas_call(
        _out_proj_body,
        out_shape=(jax.ShapeDtypeStruct((s, d), F32), jax.ShapeDtypeStruct((s, d), BF16)),
        grid=(s // tm,),
        in_specs=[row, pl.BlockSpec((d, d), lambda i: (0, 0)), row,
                  pl.BlockSpec((1, d), lambda i: (0, 0))],
        out_specs=(row, row),
        compiler_params=_cparams(("parallel",)),
        name="out_proj",
    )(merged, w_out, x, g.reshape(1, d))


HALO = 8


def _ffn_up_body(h_ref, wg_ref, wv_ref, cwg_ref, cwv_ref, cbg_ref, cbv_ref, o_ref,
                 ug_ref, uv_ref, *, tm):
    i = pl.program_id(1)

    @pl.when(i == 0)
    def _():
        ug_ref[0:HALO, :] = jnp.zeros((HALO, ug_ref.shape[1]), F32)
        uv_ref[0:HALO, :] = jnp.zeros((HALO, uv_ref.shape[1]), F32)

    h = h_ref[...]
    ug_ref[HALO:HALO + tm, :] = _dot(h, wg_ref[...])
    uv_ref[HALO:HALO + tm, :] = _dot(h, wv_ref[...])

    def conv(u_ref, cw_ref, cb_ref):
        out = cb_ref[...] + u_ref[HALO - 2:HALO - 2 + tm, :] * cw_ref[0:1, :]
        out = out + u_ref[HALO - 1:HALO - 1 + tm, :] * cw_ref[1:2, :]
        return out + u_ref[HALO:HALO + tm, :] * cw_ref[2:3, :]

    g = conv(ug_ref, cwg_ref, cbg_ref)
    val = conv(uv_ref, cwv_ref, cbv_ref)
    o_ref[...] = (jax.nn.silu(g) * val).astype(o_ref.dtype)
    ug_ref[0:HALO, :] = ug_ref[tm:tm + HALO, :]
    uv_ref[0:HALO, :] = uv_ref[tm:tm + HALO, :]


def _ffn_up(h2, wg, wv, cwg, cwv, cbg, cbv, tm, tn):
    s, d = h2.shape
    n = wg.shape[1]
    wspec = pl.BlockSpec((d, tn), lambda j, i: (0, j))
    cwspec = pl.BlockSpec((3, tn), lambda j, i: (0, j))
    cbspec = pl.BlockSpec((1, tn), lambda j, i: (0, j))
    return pl.pallas_call(
        functools.partial(_ffn_up_body, tm=tm),
        out_shape=jax.ShapeDtypeStruct((s, n), BF16),
        grid=(n // tn, s // tm),
        in_specs=[pl.BlockSpec((tm, d), lambda j, i: (i, 0)),
                  wspec, wspec, cwspec, cwspec, cbspec, cbspec],
        out_specs=pl.BlockSpec((tm, tn), lambda j, i: (i, j)),
        scratch_shapes=[pltpu.VMEM((tm + HALO, tn), F32), pltpu.VMEM((tm + HALO, tn), F32)],
        compiler_params=_cparams(("parallel", "arbitrary")),
        name="ffn_up_conv_gate",
    )(h2, wg, wv, cwg, cwv, cbg, cbv)


def _rope_tables(positions, head_dim):
    half = head_dim // 2
    inv = ROPE_THETA ** (-jnp.arange(half, dtype=F32) / half)
    ang = positions.astype(F32)[:, None] * inv
    cos = jnp.cos(ang)
    sin = jnp.sin(ang)
    reps = LANES // head_dim
    cos_t = jnp.tile(jnp.concatenate([cos, cos], axis=1), (1, reps))
    sin_t = jnp.tile(jnp.concatenate([-sin, sin], axis=1), (1, reps))
    return cos_t, sin_t


def _pad_cols(a, n):
    return jnp.pad(a, ((0, 0), (0, n - a.shape[1])))


def _layer(x, mem, positions, attn_norm, mem_norm, w_in, b_gate, dsa_q_norm, dsa_k_norm,
           mem_q_norm, mem_k_norm, w_mem_kv, w_branch, w_out, ffn_norm, w_up, conv_w,
           conv_b, w_down):
    s, d = x.shape
    assert d == D_MODEL and s % KEY_TILE == 0
    topk = min(TOPK_MAX, s // 4)
    tm = min(1024, s)
    ta = 256
    hw = HEADS * HEAD_DIM

    c_qa, c_ka, c_va, c_qb, c_kb, c_vb, c_qm = (k * hw for k in range(7))
    c_qi = 7 * hw
    c_ki = c_qi + HEADS * IDX_HEAD_DIM
    c_wi = c_ki + IDX_HEAD_DIM
    c_gl = c_wi + HEADS
    wb16 = w_in.astype(BF16)
    w_plain = jnp.concatenate([wb16[:, c_qa:c_qb], wb16[:, c_vb:c_qm]], axis=1)
    w_dsa_qk = wb16[:, c_qb:c_vb]
    w_qm = wb16[:, c_qm:c_qi]
    w_ki = wb16[:, c_ki:c_wi]
    w_idx = jnp.concatenate([wb16[:, c_qi:c_ki], w_ki, w_ki], axis=1)
    w_wi = _pad_cols(wb16[:, c_wi:c_gl], LANES)
    w_gate = wb16[:, c_gl:]

    cos_d, sin_d = _rope_tables(positions, HEAD_DIM)
    cos_i, sin_i = _rope_tables(positions, IDX_HEAD_DIM)

    h = _rmsnorm(x, attn_norm, min(512, s))

    p = _matmul(h, w_plain, _epi_plain, out_dtype=BF16, tm=tm, tn=1024, name="proj_plain")
    g_dsa = jnp.concatenate([jnp.tile(dsa_q_norm, HEADS), jnp.tile(dsa_k_norm, HEADS)])
    qkb = _matmul(h, w_dsa_qk, _epi_headnorm_rope,
                  (g_dsa.reshape(1, -1), cos_d, sin_d),
                  (_col_spec(1024), _row_spec(tm, LANES), _row_spec(tm, LANES)),
                  out_dtype=BF16, tm=tm, tn=1024, name="proj_dsa_qk")
    qm = _matmul(h, w_qm, _epi_headnorm_mem,
                 (jnp.tile(mem_q_norm, MEM_HEADS).reshape(1, -1),), (_col_spec(1024),),
                 out_dtype=BF16, tm=tm, tn=1024, name="proj_mem_q")
    qki = _matmul(h, w_idx, _epi_rope_idx, (cos_i, sin_i),
                  (_row_spec(tm, LANES), _row_spec(tm, LANES)),
                  out_dtype=BF16, tm=tm, tn=5 * LANES, name="proj_idx_qk")
    wi = _matmul(h, w_wi, functools.partial(_epi_scale, scale=HEADS ** -0.5 * IDX_HEAD_DIM ** -0.5),
                 out_dtype=F32, tm=tm, tn=LANES, name="proj_idx_w")
    gates = _matmul(h, w_gate, _epi_gate, (b_gate.reshape(1, -1),), (_col_spec(1024),),
                    out_dtype=F32, tm=tm, tn=1024, name="proj_gates")

    o_sb = _stick_breaking(p, s, ta)

    bias = _dsa_select(qki, wi, s, ta, topk)
    o_dsa = _dsa_attention(qkb, p, bias, s, min(DSA_TQ, s), min(DSA_TK, s))

    n_mem = mem.shape[0]
    mh = _rmsnorm(mem, mem_norm, n_mem)
    wkv16 = w_mem_kv.astype(BF16)
    mk = _matmul(mh, wkv16[:, :hw], _epi_headnorm_mem,
                 (jnp.tile(mem_k_norm, MEM_HEADS).reshape(1, -1),), (_col_spec(1024),),
                 out_dtype=BF16, tm=n_mem, tn=1024, name="proj_mem_k")
    mv = _matmul(mh, wkv16[:, hw:], _epi_plain, out_dtype=BF16, tm=n_mem, tn=1024,
                 name="proj_mem_v")
    o_mem = _mem_attention(qm, mk, mv, min(512, s))

    merged = _merge(o_sb, o_dsa, o_mem, w_branch.astype(BF16), gates, min(512, s), 1024)
    x1, h2 = _out_proj(merged, w_out.astype(BF16), x, ffn_norm, 256)

    wu16 = w_up.astype(BF16)
    wg = _pad_cols(wu16[:, :D_FF], D_FF_PAD)
    wv = _pad_cols(wu16[:, D_FF:], D_FF_PAD)
    cwg = _pad_cols(conv_w[:, :D_FF], D_FF_PAD)
    cwv = _pad_cols(conv_w[:, D_FF:], D_FF_PAD)
    cbg = _pad_cols(conv_b[None, :D_FF], D_FF_PAD)
    cbv = _pad_cols(conv_b[None, D_FF:], D_FF_PAD)
    act = _ffn_up(h2, wg, wv, cwg, cwv, cbg, cbv, tm, 512)
    wd = jnp.pad(w_down.astype(BF16), ((0, D_FF_PAD - D_FF), (0, 0)))
    return _matmul(act, wd, _epi_residual, (x1,),
                   (pl.BlockSpec((tm, 1024), lambda i, j, k: (i, j)),),
                   out_dtype=F32, tm=tm, tn=1024, tk=D_FF_PAD // 4, name="ffn_down")


def kernel(x, mem, positions, attn_norm, mem_norm, w_in, b_gate, dsa_q_norm, dsa_k_norm,
           mem_q_norm, mem_k_norm, w_mem_kv, w_branch, w_out, ffn_norm, w_up, conv_w,
           conv_b, w_down):
    assert x.shape[0] == 1 and attn_norm.shape[0] == 1
    y = _layer(x[0], mem[0], positions[0], attn_norm[0], mem_norm[0], w_in[0], b_gate[0],
               dsa_q_norm[0], dsa_k_norm[0], mem_q_norm[0], mem_k_norm[0], w_mem_kv[0],
               w_branch[0], w_out[0], ffn_norm[0], w_up[0], conv_w[0], conv_b[0], w_down[0])
    return y[None]
```

```python
import functools

import numpy as np
import jax
import jax.numpy as jnp
from jax import lax
from jax.experimental import pallas as pl
from jax.experimental.pallas import tpu as pltpu

F32 = jnp.float32
BF16 = jnp.bfloat16
I32 = jnp.int32

D_MODEL = 2048
CHUNK = 64
CHUNK_SHIFT = 6
HEADS = 8
HEAD_DIM = 128
IDX_HEAD_DIM = 64
TOPK_MAX = 256
MEM_HEADS = 4
MEM_HEAD_DIM = 256
D_FF = 5504
D_FF_PAD = 5632
ROPE_THETA = 10000.0
NORM_EPS = 1e-6
LANES = 128
VMEM_LIMIT = 56 * 1024 * 1024

LOG2_E = 1.4426950408889634
NEG_BIG = -1e30
BIAS_MASKED = -2e30
KEY_TILE = 512
DSA_TQ = 512
DSA_TK = 1024
SB_TAIL_CUTOFF = 104.0


def _cparams(sem):
    return pltpu.CompilerParams(dimension_semantics=sem, vmem_limit_bytes=VMEM_LIMIT)


def _dot(a, b):
    return jnp.dot(a, b, preferred_element_type=F32)


def _dot_nt(a, b):
    return lax.dot_general(a, b, (((1,), (1,)), ((), ())), preferred_element_type=F32)


def _rmsnorm_body(x_ref, g_ref, o_ref):
    x = x_ref[...]
    ms = jnp.mean(x * x, axis=-1, keepdims=True)
    o_ref[...] = ((x * lax.rsqrt(ms + NORM_EPS)) * g_ref[...]).astype(o_ref.dtype)


def _rmsnorm(x, g, tm):
    m, d = x.shape
    return pl.pallas_call(
        _rmsnorm_body,
        out_shape=jax.ShapeDtypeStruct((m, d), BF16),
        grid=(m // tm,),
        in_specs=[pl.BlockSpec((tm, d), lambda i: (i, 0)),
                  pl.BlockSpec((1, d), lambda i: (0, 0))],
        out_specs=pl.BlockSpec((tm, d), lambda i: (i, 0)),
        compiler_params=_cparams(("parallel",)),
        name="rmsnorm",
    )(x, g.reshape(1, d))


def _epi_plain(y, o_ref):
    o_ref[...] = y.astype(o_ref.dtype)


def _epi_headnorm_rope(y, o_ref, g_ref, cos_ref, sin_ref):
    cos = cos_ref[...]
    sin = sin_ref[...]
    heads = [slice(h * HEAD_DIM, (h + 1) * HEAD_DIM) for h in range(y.shape[1] // HEAD_DIM)]
    ms = [jnp.mean(y[:, sl] * y[:, sl], axis=-1, keepdims=True) for sl in heads]
    yn = [(y[:, sl] * lax.rsqrt(m + NORM_EPS)) * g_ref[:, sl] for sl, m in zip(heads, ms)]
    rot = [pltpu.roll(v, HEAD_DIM // 2, 1) for v in yn]
    for sl, v, r in zip(heads, yn, rot):
        o_ref[:, sl] = (v * cos + r * sin).astype(o_ref.dtype)


def _epi_headnorm_mem(y, o_ref, g_ref):
    heads = [slice(h * MEM_HEAD_DIM, (h + 1) * MEM_HEAD_DIM)
             for h in range(y.shape[1] // MEM_HEAD_DIM)]
    ms = [jnp.mean(y[:, sl] * y[:, sl], axis=-1, keepdims=True) for sl in heads]
    for sl, m in zip(heads, ms):
        o_ref[:, sl] = ((y[:, sl] * lax.rsqrt(m + NORM_EPS)) * g_ref[:, sl]).astype(o_ref.dtype)


def _epi_rope_idx(y, o_ref, cos_ref, sin_ref):
    cos = cos_ref[...]
    sin = sin_ref[...]
    lane = lax.broadcasted_iota(I32, cos.shape, 1)
    first_half = (lane % IDX_HEAD_DIM) < (IDX_HEAD_DIM // 2)
    tiles = [slice(t * LANES, (t + 1) * LANES) for t in range(y.shape[1] // LANES)]
    up = [pltpu.roll(y[:, sl], LANES - IDX_HEAD_DIM // 2, 1) for sl in tiles]
    down = [pltpu.roll(y[:, sl], IDX_HEAD_DIM // 2, 1) for sl in tiles]
    for sl, u, d in zip(tiles, up, down):
        o_ref[:, sl] = (y[:, sl] * cos + jnp.where(first_half, u, d) * sin).astype(o_ref.dtype)


def _epi_scale(y, o_ref, *, scale):
    o_ref[...] = (y * scale).astype(o_ref.dtype)


def _epi_gate(y, o_ref, b_ref):
    o_ref[...] = jax.nn.sigmoid(y + b_ref[...]).astype(o_ref.dtype)


def _epi_residual(y, o_ref, r_ref):
    o_ref[...] = (r_ref[...] + y).astype(o_ref.dtype)


def _matmul_body(a_ref, w_ref, *rest, epi, n_extra, nk):
    extras = rest[:n_extra]
    o_ref = rest[n_extra]
    if nk == 1:
        epi(_dot(a_ref[...], w_ref[...]), o_ref, *extras)
        return
    acc_ref = rest[n_extra + 1]
    k = pl.program_id(2)

    @pl.when(k == 0)
    def _():
        acc_ref[...] = jnp.zeros_like(acc_ref)

    acc_ref[...] += _dot(a_ref[...], w_ref[...])

    @pl.when(k == nk - 1)
    def _():
        epi(acc_ref[...], o_ref, *extras)


def _matmul(a, w, epi, extras=(), extra_specs=(), *, out_dtype, tm, tn, tk=None, name):
    m, kdim = a.shape
    n = w.shape[1]
    tk = kdim if tk is None else tk
    nk = kdim // tk
    assert m % tm == 0 and n % tn == 0 and kdim % tk == 0
    scratch = [pltpu.VMEM((tm, tn), F32)] if nk > 1 else []
    return pl.pallas_call(
        functools.partial(_matmul_body, epi=epi, n_extra=len(extras), nk=nk),
        out_shape=jax.ShapeDtypeStruct((m, n), out_dtype),
        grid=(m // tm, n // tn, nk),
        in_specs=[pl.BlockSpec((tm, tk), lambda i, j, k: (i, k)),
                  pl.BlockSpec((tk, tn), lambda i, j, k: (k, j))] + list(extra_specs),
        out_specs=pl.BlockSpec((tm, tn), lambda i, j, k: (i, j)),
        scratch_shapes=scratch,
        compiler_params=_cparams(("parallel", "parallel", "arbitrary")),
        name=name,
    )(a, w, *extras)


def _row_spec(tm, width):
    return pl.BlockSpec((tm, width), lambda i, j, k: (i, 0))


def _col_spec(tn):
    return pl.BlockSpec((1, tn), lambda i, j, k: (0, j))


SB_GROUP = 4


def _sb_body(q_ref, k_ref, v_ref, o_ref, acc_ref, r_ref, *, t, scale):
    i = pl.program_id(1)
    row = lax.broadcasted_iota(I32, (t, t), 0)
    col = lax.broadcasted_iota(I32, (t, t), 1)
    upper = jnp.where(row > col, 1.0, 0.0).astype(BF16)
    visible = col < row

    def block(kb, diagonal):
        start = pl.multiple_of(kb * t, t)
        heads = [slice(g * HEAD_DIM, (g + 1) * HEAD_DIM) for g in range(SB_GROUP)]
        z = [_dot_nt(q_ref[:, sl], k_ref[pl.ds(start, t), sl]) * scale for sl in heads]
        l1p = [jnp.log1p(jnp.exp(-jnp.abs(zz))) for zz in z]
        sp = [jnp.maximum(zz, 0.0) + ll for zz, ll in zip(z, l1p)]
        logsig = [jnp.minimum(zz, 0.0) - ll for zz, ll in zip(z, l1p)]
        if diagonal:
            sp = [jnp.where(visible, x, 0.0) for x in sp]
        hi = [x.astype(BF16) for x in sp]
        lo = [(x - hh.astype(F32)).astype(BF16) for x, hh in zip(sp, hi)]
        within = [_dot(hh, upper) + _dot(ll, upper) for hh, ll in zip(hi, lo)]
        r_old = [r_ref[g] for g in range(SB_GROUP)]
        a = [jnp.exp(ls - (ro + w)) for ls, ro, w in zip(logsig, r_old, within)]
        if diagonal:
            a = [jnp.where(visible, x, 0.0) for x in a]
        r_min = None
        for g, sl in enumerate(heads):
            acc_ref[:, sl] += _dot(a[g].astype(BF16), v_ref[pl.ds(start, t), sl])
            r_new = r_old[g] + jnp.sum(sp[g], axis=-1, keepdims=True)
            r_ref[g] = r_new
            r_head = jnp.min(r_new)
            r_min = r_head if r_min is None else jnp.minimum(r_min, r_head)
        return r_min

    acc_ref[...] = jnp.zeros_like(acc_ref)
    r_ref[...] = jnp.zeros_like(r_ref)
    r_min = block(i, True)

    def cond(c):
        kb, r_min = c
        return jnp.logical_and(kb >= 0, r_min < SB_TAIL_CUTOFF)

    def body(c):
        kb, _ = c
        return kb - 1, block(kb, False)

    lax.while_loop(cond, body, (i - 1, r_min))
    o_ref[...] = acc_ref[...].astype(o_ref.dtype)


def _stick_breaking(p, s, t):
    groups = HEADS // SB_GROUP
    width = SB_GROUP * HEAD_DIM
    return pl.pallas_call(
        functools.partial(_sb_body, t=t, scale=HEAD_DIM ** -0.5),
        out_shape=jax.ShapeDtypeStruct((s, HEADS * HEAD_DIM), BF16),
        grid=(groups, s // t),
        in_specs=[pl.BlockSpec((t, width), lambda h, i: (i, h)),
                  pl.BlockSpec((s, width), lambda h, i: (0, groups + h),
                               pipeline_mode=pl.Buffered(1)),
                  pl.BlockSpec((s, width), lambda h, i: (0, 2 * groups + h),
                               pipeline_mode=pl.Buffered(1))],
        out_specs=pl.BlockSpec((t, width), lambda h, i: (i, h)),
        scratch_shapes=[pltpu.VMEM((t, width), F32), pltpu.VMEM((SB_GROUP, t, 1), F32)],
        compiler_params=_cparams(("parallel", "parallel")),
        name="stick_breaking",
    )(p, p, p)


def _split_index_heads(qi_ref):
    lane = lax.broadcasted_iota(I32, (qi_ref.shape[0], LANES), 1)
    halves = []
    for pair in range(HEADS // 2):
        qp = qi_ref[:, pair * LANES:(pair + 1) * LANES]
        zero = jnp.zeros_like(qp)
        halves.append(jnp.where(lane < IDX_HEAD_DIM, qp, zero))
        halves.append(jnp.where(lane >= IDX_HEAD_DIM, qp, zero))
    return halves


def _index_scores(q_halves, kk, wcols):
    score = None
    for h in range(HEADS):
        term = jnp.maximum(_dot_nt(q_halves[h], kk), 0.0) * wcols[h]
        score = term if score is None else score + term
    return jnp.where(score == 0.0, 0.0, score)


def _admissible(q0, k0, shape):
    qpos = q0 + lax.broadcasted_iota(I32, shape, 0)
    kpos = k0 + lax.broadcasted_iota(I32, shape, 1)
    return (kpos >> CHUNK_SHIFT) <= (qpos >> CHUNK_SHIFT), kpos


def _ordered_key(x):
    b = lax.bitcast_convert_type(x, I32)
    return b ^ ((b >> 31) & jnp.int32(0x7FFFFFFF))


KEY_NEG_INF = int(np.array(-np.inf, np.float32).view(np.int32) ^ 0x7FFFFFFF)
ROW_GROUP = 64


def _key_to_float(k):
    return lax.bitcast_convert_type(k ^ ((k >> 31) & jnp.int32(0x7FFFFFFF)), F32)


def _any(mask):
    return jnp.max(jnp.where(mask, 1.0, 0.0)) > 0.5


def _dsa_select_body(qi_ref, kk_ref, w_ref, bias_ref, key_ref, mx_ref, mn_ref, st_ref,
                     *, t, topk, s_total):
    i = pl.program_id(0)
    n_tiles = (i * t + t + KEY_TILE - 1) // KEY_TILE
    n_all = s_total // KEY_TILE
    lane_tiles = KEY_TILE // LANES
    w = w_ref[...]
    wcols = [w[:, h:h + 1] for h in range(HEADS)]
    q_halves = _split_index_heads(qi_ref)
    (lo_ref, hi_ref, clo_ref, chi_ref, aux_ref, cnt_ref, pos_ref, nonneg_ref) = (
        st_ref.at[k] for k in range(8))
    shape = (t, LANES)
    mx_ref[...] = jnp.full(shape, -jnp.inf, F32)
    mn_ref[...] = jnp.full(shape, jnp.inf, F32)
    pos_ref[...] = jnp.zeros(shape, I32)
    nonneg_ref[...] = jnp.zeros(shape, I32)

    def fill(c, carry, *, masked):
        start = pl.multiple_of(c * KEY_TILE, KEY_TILE)
        score = _index_scores(q_halves, kk_ref[pl.ds(start, KEY_TILE), :], wcols)
        if masked:
            adm, _ = _admissible(i * t, c * KEY_TILE, (t, KEY_TILE))
            low = jnp.where(adm, score, -jnp.inf)
            high = jnp.where(adm, score, jnp.inf)
        else:
            low = high = score
        key_ref[c] = _ordered_key(low)
        mx = mx_ref[...]
        mn = mn_ref[...]
        pos = pos_ref[...]
        nonneg = nonneg_ref[...]
        for sub in range(lane_tiles):
            sl = slice(sub * LANES, (sub + 1) * LANES)
            mx = jnp.maximum(mx, low[:, sl])
            mn = jnp.minimum(mn, high[:, sl])
            pos = pos + jnp.where(low[:, sl] > 0.0, 1, 0).astype(I32)
            nonneg = nonneg + jnp.where(low[:, sl] >= 0.0, 1, 0).astype(I32)
        mx_ref[...] = mx
        mn_ref[...] = mn
        pos_ref[...] = pos
        nonneg_ref[...] = nonneg
        return carry

    n_full = (i * (t // CHUNK) + 1) // (KEY_TILE // CHUNK)
    lax.fori_loop(0, n_full, functools.partial(fill, masked=False), 0)
    lax.fori_loop(n_full, n_tiles, functools.partial(fill, masked=True), 0)

    groups =[slice(g * ROW_GROUP, (g + 1) * ROW_GROUP) for g in range(t // ROW_GROUP)]
    col_in_tile = lax.broadcasted_iota(I32, (ROW_GROUP, LANES), 1)

    def lane_total(x):
        return jnp.broadcast_to(jnp.sum(x, axis=1, keepdims=True), shape)

    def count(pred):
        for rows in groups:
            lo_g = lo_ref[rows, :]
            aux_g = aux_ref[rows, :]

            def tile(c, cnt):
                keys = key_ref[c, rows, :]
                for sub in range(lane_tiles):
                    kcol = c * KEY_TILE + sub * LANES + col_in_tile
                    hit = pred(keys[:, sub * LANES:(sub + 1) * LANES], kcol, lo_g, aux_g)
                    cnt = cnt + jnp.where(hit, 1, 0).astype(I32)
                return cnt
            cnt_ref[rows, :] = lax.fori_loop(0, n_tiles, tile,
                                             jnp.zeros((ROW_GROUP, LANES), I32))
        return lane_total(cnt_ref[...])

    row_max = jnp.broadcast_to(jnp.max(mx_ref[...], axis=1, keepdims=True), shape)
    row_min = jnp.broadcast_to(jnp.min(mn_ref[...], axis=1, keepdims=True), shape)
    n_pos = lane_total(pos_ref[...])
    n_nonneg = lane_total(nonneg_ref[...])
    qpos = i * t + lax.broadcasted_iota(I32, shape, 0)
    n_adm = ((qpos >> CHUNK_SHIFT) + 1) * CHUNK
    enough = n_adm >= topk

    def active(lo, hi, c_lo):
        return jnp.logical_and(c_lo != topk, hi > lo + 1)

    positive = n_pos >= topk
    negative = n_nonneg < topk
    lo0 = jnp.where(positive, 1, jnp.where(negative, _ordered_key(row_min), 0))
    c_lo0 = jnp.where(positive, n_pos, jnp.where(negative, n_adm, n_nonneg))
    hi0 = jnp.where(positive, _ordered_key(row_max) + 1, jnp.where(negative, 0, 1))
    c_hi0 = jnp.where(positive, 0, jnp.where(negative, n_nonneg, n_pos))
    lo_ref[...] = jnp.where(enough, lo0, KEY_NEG_INF)
    hi_ref[...] = jnp.where(enough, hi0, KEY_NEG_INF + 1)
    clo_ref[...] = jnp.where(enough, c_lo0, topk)
    chi_ref[...] = jnp.where(enough, c_hi0, 0)

    def step(geometric):
        lo = lo_ref[...]
        hi = hi_ref[...]
        c_lo = clo_ref[...]
        act = active(lo, hi, c_lo)
        key_mid = (lo >> 1) + (hi >> 1) + (lo & hi & 1)
        if geometric:
            cand = key_mid
        else:
            mid = _ordered_key(0.5 * _key_to_float(lo) + 0.5 * _key_to_float(hi))
            cand = jnp.where(jnp.logical_and(mid > lo, mid < hi), mid, key_mid)
        aux_ref[...] = cand
        cnt = count(lambda keys, kcol, lo_g, cand_g: keys >= cand_g)
        up = jnp.logical_and(act, cnt >= topk)
        down = jnp.logical_and(act, cnt < topk)
        lo_ref[...] = jnp.where(up, cand, lo)
        clo_ref[...] = jnp.where(up, cnt, c_lo)
        hi_ref[...] = jnp.where(down, cand, hi)
        chi_ref[...] = jnp.where(down, cnt, chi_ref[...])

    def any_active():
        return _any(active(lo_ref[...], hi_ref[...], clo_ref[...])).astype(I32)

    def three_steps(_):
        step(False)
        step(False)
        step(True)
        return any_active()

    lax.while_loop(lambda go: go > 0, three_steps, any_active())

    need = topk - chi_ref[...]
    tie = clo_ref[...] > topk

    def tie_cut():
        def idx_step(b, p):
            cand = p | (jnp.int32(1) << (s_total.bit_length() - 1 - b))
            aux_ref[...] = cand
            cnt = count(lambda keys, kcol, thr_g, cand_g:
                        jnp.logical_and(keys == thr_g, kcol < cand_g))
            return jnp.where(cnt < need, cand, p)
        p = lax.fori_loop(0, s_total.bit_length(), idx_step, jnp.zeros(shape, I32))
        aux_ref[...] = jnp.where(tie, p, s_total)

    aux_ref[...] = jnp.full(shape, s_total, I32)
    has_tie = _any(tie)
    pl.when(has_tie)(tie_cut)

    def emit_masks(with_ties):
        for rows in groups:
            thr = lo_ref[rows, :]
            jcut = aux_ref[rows, :]
            floor = jnp.maximum(thr, KEY_NEG_INF + 1)

            def emit(c, carry):
                keys = key_ref[c, rows, :]
                parts = []
                for sub in range(lane_tiles):
                    ksub = keys[:, sub * LANES:(sub + 1) * LANES]
                    if with_ties:
                        kcol = c * KEY_TILE + sub * LANES + col_in_tile
                        chosen = jnp.logical_or(ksub > thr,
                                                jnp.logical_and(ksub == thr, kcol <= jcut))
                        sel = jnp.logical_and(chosen, ksub > KEY_NEG_INF)
                    else:
                        sel = ksub >= floor
                    parts.append(jnp.where(sel, 0.0, BIAS_MASKED))
                bias_ref[0, c, rows, :] = jnp.concatenate(parts, axis=1).astype(BF16)
                return carry

            lax.fori_loop(0, n_tiles, emit, 0)

    pl.when(has_tie)(functools.partial(emit_masks, True))
    pl.when(jnp.logical_not(has_tie))(functools.partial(emit_masks, False))

    def pad(c, carry):
        bias_ref[0, c] = jnp.full((t, KEY_TILE), BIAS_MASKED, BF16)
        return carry

    lax.fori_loop(n_tiles, n_all, pad, 0)


def _dsa_select(qki, wi, s, t, topk):
    n_all = s // KEY_TILE
    return pl.pallas_call(
        functools.partial(_dsa_select_body, t=t, topk=topk, s_total=s),
        out_shape=jax.ShapeDtypeStruct((s // t, n_all, t, KEY_TILE), BF16),
        grid=(s // t,),
        in_specs=[pl.BlockSpec((t, 4 * LANES), lambda i: (i, 0)),
                  pl.BlockSpec((s, LANES), lambda i: (0, 4)),
                  pl.BlockSpec((t, LANES), lambda i: (i, 0))],
        out_specs=pl.BlockSpec((1, n_all, t, KEY_TILE), lambda i: (i, 0, 0, 0)),
        scratch_shapes=[pltpu.VMEM((n_all, t, KEY_TILE), I32),
                        pltpu.VMEM((t, LANES), F32), pltpu.VMEM((t, LANES), F32),
                        pltpu.VMEM((8, t, LANES), I32)],
        compiler_params=_cparams(("parallel",)),
        name="dsa_select",
    )(qki, qki, wi)


ATT_ROWS = 64


def _dsa_attn_body(qi_of, kj_of, last_of, q_ref, k_ref, v_ref, bias_ref, o_ref,
                   acc_ref, m_ref, l_ref, z_ref, p_ref, b_ref, *, scale):
    step = pl.program_id(0)
    tq, tk = z_ref.shape[1:]

    @pl.when(kj_of[step] == 0)
    def _():
        acc_ref[...] = jnp.zeros_like(acc_ref)
        m_ref[...] = jnp.full_like(m_ref, NEG_BIG)
        l_ref[...] = jnp.zeros_like(l_ref)

    c2 = scale * LOG2_E
    sel_t = bias_ref.shape[2]
    for a in range(bias_ref.shape[0]):
        for b in range(bias_ref.shape[1]):
            b_ref[a * sel_t:(a + 1) * sel_t, b * KEY_TILE:(b + 1) * KEY_TILE] = (
                bias_ref[a, b].astype(F32))

    ones = jnp.ones((tk, HEAD_DIM), BF16)
    chunks = [slice(c * ATT_ROWS, (c + 1) * ATT_ROWS) for c in range(tq // ATT_ROWS)]
    def logits(h):
        sl = slice(h * HEAD_DIM, (h + 1) * HEAD_DIM)
        s = _dot_nt(q_ref[:, sl], k_ref[:, sl]) * c2 + b_ref[...]
        z_ref[h % 2] = s
        m_old = m_ref[h]
        return m_old, jnp.maximum(m_old, jnp.max(s, axis=-1, keepdims=True))

    ahead = logits(0)
    for h in range(HEADS):
        sl = slice(h * HEAD_DIM, (h + 1) * HEAD_DIM)
        buf = h % 2
        m_old, m_new = ahead
        if h + 1 < HEADS:
            ahead = logits(h + 1)
        for rows in chunks:
            p_ref[buf, rows, :] = jnp.exp2(z_ref[buf, rows, :] - m_new[rows]).astype(BF16)
        alpha = jnp.exp2(m_old - m_new)
        pv = _dot(p_ref[buf], jnp.concatenate([v_ref[:, sl], ones], axis=1))
        acc_ref[:, sl] = alpha * acc_ref[:, sl] + pv[:, :HEAD_DIM]
        l_ref[h] = alpha * l_ref[h] + pv[:, HEAD_DIM:]
        m_ref[h] = m_new

    @pl.when(last_of[step] == 1)
    def _():
        for h in range(HEADS):
            sl = slice(h * HEAD_DIM, (h + 1) * HEAD_DIM)
            o_ref[:, sl] = (acc_ref[:, sl] / l_ref[h]).astype(o_ref.dtype)


def _dsa_attention(qkb, p, bias, s, tq, tk):
    sel_t = bias.shape[2]
    qi_of, kj_of, last_of = [], [], []
    for q in range(s // tq):
        n_tiles = (q * tq + tq + tk - 1) // tk
        qi_of += [q] * n_tiles
        kj_of += list(range(n_tiles))
        last_of += [0] * (n_tiles - 1) + [1]
    width = HEADS * HEAD_DIM
    grid_spec = pltpu.PrefetchScalarGridSpec(
        num_scalar_prefetch=3,
        grid=(len(qi_of),),
        in_specs=[
            pl.BlockSpec((tq, width), lambda g, qo, ko, lo: (qo[g], 0)),
            pl.BlockSpec((tk, width), lambda g, qo, ko, lo: (ko[g], 1)),
            pl.BlockSpec((tk, width), lambda g, qo, ko, lo: (ko[g], 3)),
            pl.BlockSpec((tq // sel_t, tk // KEY_TILE, sel_t, KEY_TILE),
                         lambda g, qo, ko, lo: (qo[g], ko[g], 0, 0)),
        ],
        out_specs=pl.BlockSpec((tq, width), lambda g, qo, ko, lo: (qo[g], 0)),
        scratch_shapes=[pltpu.VMEM((tq, width), F32),
                        pltpu.VMEM((HEADS, tq, 1), F32),
                        pltpu.VMEM((HEADS, tq, HEAD_DIM), F32),
                        pltpu.VMEM((2, tq, tk), F32),
                        pltpu.VMEM((2, tq, tk), BF16),
                        pltpu.VMEM((tq, tk), F32)],
    )
    to_i32 = lambda v: jnp.asarray(np.asarray(v, np.int32))
    return pl.pallas_call(
        functools.partial(_dsa_attn_body, scale=HEAD_DIM ** -0.5),
        out_shape=jax.ShapeDtypeStruct((s, width), BF16),
        grid_spec=grid_spec,
        compiler_params=_cparams(("arbitrary",)),
        name="dsa_attention",
    )(to_i32(qi_of), to_i32(kj_of), to_i32(last_of), qkb, qkb, p, bias)


def _mem_attn_body(q_ref, mk_ref, mv_ref, o_ref, *, scale):
    heads = [slice(h * MEM_HEAD_DIM, (h + 1) * MEM_HEAD_DIM) for h in range(MEM_HEADS)]
    sc = [_dot_nt(q_ref[:, sl], mk_ref[:, sl]) * scale for sl in heads]
    e = [jnp.exp(x - jnp.max(x, axis=-1, keepdims=True)) for x in sc]
    p = [x / jnp.sum(x, axis=-1, keepdims=True) for x in e]
    for sl, pp in zip(heads, p):
        o_ref[:, sl] = _dot(pp.astype(BF16), mv_ref[:, sl]).astype(o_ref.dtype)


def _mem_attention(qm, mk, mv, tm):
    s, width = qm.shape
    n_mem = mk.shape[0]
    return pl.pallas_call(
        functools.partial(_mem_attn_body, scale=MEM_HEAD_DIM ** -0.5),
        out_shape=jax.ShapeDtypeStruct((s, width), BF16),
        grid=(s // tm,),
        in_specs=[pl.BlockSpec((tm, width), lambda i: (i, 0)),
                  pl.BlockSpec((n_mem, width), lambda i: (0, 0)),
                  pl.BlockSpec((n_mem, width), lambda i: (0, 0))],
        out_specs=pl.BlockSpec((tm, width), lambda i: (i, 0)),
        compiler_params=_cparams(("parallel",)),
        name="mem_attention",
    )(qm, mk, mv)


def _merge_body(o0_ref, o1_ref, o2_ref, w_ref, g0_ref, g1_ref, g2_ref, out_ref):
    merged = (g0_ref[...] * _dot(o0_ref[...], w_ref[0])
              + g1_ref[...] * _dot(o1_ref[...], w_ref[1])
              + g2_ref[...] * _dot(o2_ref[...], w_ref[2]))
    out_ref[...] = merged.astype(out_ref.dtype)


def _merge(o_sb, o_dsa, o_mem, w_branch, gates, tm, tn):
    s, bw = o_sb.shape
    d = w_branch.shape[2]
    nj = d // tn
    o_spec = pl.BlockSpec((tm, bw), lambda j, i: (i, 0))
    gate_specs = [pl.BlockSpec((tm, tn), functools.partial(lambda j, i, b: (i, b * nj + j), b=b))
                  for b in range(3)]
    return pl.pallas_call(
        _merge_body,
        out_shape=jax.ShapeDtypeStruct((s, d), BF16),
        grid=(nj, s // tm),
        in_specs=[o_spec, o_spec, o_spec,
                  pl.BlockSpec((3, bw, tn), lambda j, i: (0, 0, j))] + gate_specs,
        out_specs=pl.BlockSpec((tm, tn), lambda j, i: (i, j)),
        compiler_params=_cparams(("parallel", "parallel")),
        name="gated_merge",
    )(o_sb, o_dsa, o_mem, w_branch, gates, gates, gates)


def _out_proj_body(m_ref, w_ref, x_ref, g_ref, x1_ref, h2_ref):
    x1 = x_ref[...] + _dot(m_ref[...], w_ref[...])
    x1_ref[...] = x1
    ms = jnp.mean(x1 * x1, axis=-1, keepdims=True)
    h2_ref[...] = ((x1 * lax.rsqrt(ms + NORM_EPS)) * g_ref[...]).astype(h2_ref.dtype)


def _out_proj(merged, w_out, x, g, tm):
    s, d = x.shape
    row = pl.BlockSpec((tm, d), lambda i: (i, 0))
    return pl.pallas_call(
        _out_proj_body,
        out_shape=(jax.ShapeDtypeStruct((s, d), F32), jax.ShapeDtypeStruct((s, d), BF16)),
        grid=(s // tm,),
        in_specs=[row, pl.BlockSpec((d, d), lambda i: (0, 0)), row,
                  pl.BlockSpec((1, d), lambda i: (0, 0))],
        out_specs=(row, row),
        compiler_params=_cparams(("parallel",)),
        name="out_proj",
    )(merged, w_out, x, g.reshape(1, d))


HALO = 8


def _ffn_up_body(h_ref, wg_ref, wv_ref, cwg_ref, cwv_ref, cbg_ref, cbv_ref, o_ref,
                 ug_ref, uv_ref, *, tm):
    i = pl.program_id(1)

    @pl.when(i == 0)
    def _():
        ug_ref[0:HALO, :] = jnp.zeros((HALO, ug_ref.shape[1]), F32)
        uv_ref[0:HALO, :] = jnp.zeros((HALO, uv_ref.shape[1]), F32)

    def conv(u_ref, cw_ref, cb_ref):
        out = cb_ref[...] + u_ref[HALO - 2:HALO - 2 + tm, :] * cw_ref[0:1, :]
        out = out + u_ref[HALO - 1:HALO - 1 + tm, :] * cw_ref[1:2, :]
        return out + u_ref[HALO:HALO + tm, :] * cw_ref[2:3, :]

    h = h_ref[...]
    ug_ref[HALO:HALO + tm, :] = _dot(h, wg_ref[...])
    uv_ref[HALO:HALO + tm, :] = _dot(h, wv_ref[...])
    g = conv(ug_ref, cwg_ref, cbg_ref)
    val = conv(uv_ref, cwv_ref, cbv_ref)
    o_ref[...] = (jax.nn.silu(g) * val).astype(o_ref.dtype)
    ug_ref[0:HALO, :] = ug_ref[tm:tm + HALO, :]
    uv_ref[0:HALO, :] = uv_ref[tm:tm + HALO, :]


def _ffn_up(h2, wg, wv, cwg, cwv, cbg, cbv, tm, tn):
    s, d = h2.shape
    n = wg.shape[1]
    wspec = pl.BlockSpec((d, tn), lambda j, i: (0, j))
    cwspec = pl.BlockSpec((3, tn), lambda j, i: (0, j))
    cbspec = pl.BlockSpec((1, tn), lambda j, i: (0, j))
    return pl.pallas_call(
        functools.partial(_ffn_up_body, tm=tm),
        out_shape=jax.ShapeDtypeStruct((s, n), BF16),
        grid=(n // tn, s // tm),
        in_specs=[pl.BlockSpec((tm, d), lambda j, i: (i, 0)),
                  wspec, wspec, cwspec, cwspec, cbspec, cbspec],
        out_specs=pl.BlockSpec((tm, tn), lambda j, i: (i, j)),
        scratch_shapes=[pltpu.VMEM((tm + HALO, tn), F32), pltpu.VMEM((tm + HALO, tn), F32)],
        compiler_params=_cparams(("parallel", "arbitrary")),
        name="ffn_up_conv_gate",
    )(h2, wg, wv, cwg, cwv, cbg, cbv)


def _rope_tables(positions, head_dim):
    half = head_dim // 2
    inv = ROPE_THETA ** (-jnp.arange(half, dtype=F32) / half)
    ang = positions.astype(F32)[:, None] * inv
    cos = jnp.cos(ang)
    sin = jnp.sin(ang)
    reps = LANES // head_dim
    cos_t = jnp.tile(jnp.concatenate([cos, cos], axis=1), (1, reps))
    sin_t = jnp.tile(jnp.concatenate([-sin, sin], axis=1), (1, reps))
    return cos_t, sin_t


def _pad_cols(a, n):
    return jnp.pad(a, ((0, 0), (0, n - a.shape[1])))


def _layer(x, mem, positions, attn_norm, mem_norm, w_in, b_gate, dsa_q_norm, dsa_k_norm,
           mem_q_norm, mem_k_norm, w_mem_kv, w_branch, w_out, ffn_norm, w_up, conv_w,
           conv_b, w_down):
    s, d = x.shape
    assert d == D_MODEL and s % KEY_TILE == 0
    topk = min(TOPK_MAX, s // 4)
    tm = min(1024, s)
    ta = 256
    hw = HEADS * HEAD_DIM

    c_qa, c_ka, c_va, c_qb, c_kb, c_vb, c_qm = (k * hw for k in range(7))
    c_qi = 7 * hw
    c_ki = c_qi + HEADS * IDX_HEAD_DIM
    c_wi = c_ki + IDX_HEAD_DIM
    c_gl = c_wi + HEADS
    wb16 = w_in.astype(BF16)
    w_plain = jnp.concatenate([wb16[:, c_qa:c_qb], wb16[:, c_vb:c_qm]], axis=1)
    w_dsa_qk = wb16[:, c_qb:c_vb]
    w_qm = wb16[:, c_qm:c_qi]
    w_ki = wb16[:, c_ki:c_wi]
    w_idx = jnp.concatenate([wb16[:, c_qi:c_ki], w_ki, w_ki], axis=1)
    w_wi = _pad_cols(wb16[:, c_wi:c_gl], LANES)
    w_gate = wb16[:, c_gl:]

    cos_d, sin_d = _rope_tables(positions, HEAD_DIM)
    cos_i, sin_i = _rope_tables(positions, IDX_HEAD_DIM)

    h = _rmsnorm(x, attn_norm, min(512, s))

    p = _matmul(h, w_plain, _epi_plain, out_dtype=BF16, tm=tm, tn=1024, name="proj_plain")
    g_dsa = jnp.concatenate([jnp.tile(dsa_q_norm, HEADS), jnp.tile(dsa_k_norm, HEADS)])
    qkb = _matmul(h, w_dsa_qk, _epi_headnorm_rope,
                  (g_dsa.reshape(1, -1), cos_d, sin_d),
                  (_col_spec(1024), _row_spec(tm, LANES), _row_spec(tm, LANES)),
                  out_dtype=BF16, tm=tm, tn=1024, name="proj_dsa_qk")
    qm = _matmul(h, w_qm, _epi_headnorm_mem,
                 (jnp.tile(mem_q_norm, MEM_HEADS).reshape(1, -1),), (_col_spec(1024),),
                 out_dtype=BF16, tm=tm, tn=1024, name="proj_mem_q")
    qki = _matmul(h, w_idx, _epi_rope_idx, (cos_i, sin_i),
                  (_row_spec(tm, LANES), _row_spec(tm, LANES)),
                  out_dtype=BF16, tm=tm, tn=5 * LANES, name="proj_idx_qk")
    wi = _matmul(h, w_wi, functools.partial(_epi_scale, scale=HEADS ** -0.5 * IDX_HEAD_DIM ** -0.5),
                 out_dtype=F32, tm=tm, tn=LANES, name="proj_idx_w")
    gates = _matmul(h, w_gate, _epi_gate, (b_gate.reshape(1, -1),), (_col_spec(1024),),
                    out_dtype=F32, tm=tm, tn=1024, name="proj_gates")

    o_sb = _stick_breaking(p, s, ta)

    bias = _dsa_select(qki, wi, s, ta, topk)
    o_dsa = _dsa_attention(qkb, p, bias, s, min(DSA_TQ, s), min(DSA_TK, s))

    n_mem = mem.shape[0]
    mh = _rmsnorm(mem, mem_norm, n_mem)
    wkv16 = w_mem_kv.astype(BF16)
    mk = _matmul(mh, wkv16[:, :hw], _epi_headnorm_mem,
                 (jnp.tile(mem_k_norm, MEM_HEADS).reshape(1, -1),), (_col_spec(1024),),
                 out_dtype=BF16, tm=n_mem, tn=1024, name="proj_mem_k")
    mv = _matmul(mh, wkv16[:, hw:], _epi_plain, out_dtype=BF16, tm=n_mem, tn=1024,
                 name="proj_mem_v")
    o_mem = _mem_attention(qm, mk, mv, min(512, s))

    merged = _merge(o_sb, o_dsa, o_mem, w_branch.astype(BF16), gates, min(512, s), 1024)
    x1, h2 = _out_proj(merged, w_out.astype(BF16), x, ffn_norm, 256)

    wu16 = w_up.astype(BF16)
    wg = _pad_cols(wu16[:, :D_FF], D_FF_PAD)
    wv = _pad_cols(wu16[:, D_FF:], D_FF_PAD)
    cwg = _pad_cols(conv_w[:, :D_FF], D_FF_PAD)
    cwv = _pad_cols(conv_w[:, D_FF:], D_FF_PAD)
    cbg = _pad_cols(conv_b[None, :D_FF], D_FF_PAD)
    cbv = _pad_cols(conv_b[None, D_FF:], D_FF_PAD)
    act = _ffn_up(h2, wg, wv, cwg, cwv, cbg, cbv, tm, 512)
    wd = jnp.pad(w_down.astype(BF16), ((0, D_FF_PAD - D_FF), (0, 0)))
    return _matmul(act, wd, _epi_residual, (x1,),
                   (pl.BlockSpec((tm, 1024), lambda i, j, k: (i, j)),),
                   out_dtype=F32, tm=tm, tn=1024, tk=D_FF_PAD // 2, name="ffn_down")


def kernel(x, mem, positions, attn_norm, mem_norm, w_in, b_gate, dsa_q_norm, dsa_k_norm,
           mem_q_norm, mem_k_norm, w_mem_kv, w_branch, w_out, ffn_norm, w_up, conv_w,
           conv_b, w_down):
    assert x.shape[0] == 1 and attn_norm.shape[0] == 1
    y = _layer(x[0], mem[0], positions[0], attn_norm[0], mem_norm[0], w_in[0], b_gate[0],
               dsa_q_norm[0], dsa_k_norm[0], mem_q_norm[0], mem_k_norm[0], w_mem_kv[0],
               w_branch[0], w_out[0], ffn_norm[0], w_up[0], conv_w[0], conv_b[0], w_down[0])
    return y[None]
```

```python
import functools

import numpy as np
import jax
import jax.numpy as jnp
from jax import lax
from jax.experimental import pallas as pl
from jax.experimental.pallas import tpu as pltpu

F32 = jnp.float32
BF16 = jnp.bfloat16
I32 = jnp.int32

D_MODEL = 2048
CHUNK = 64
CHUNK_SHIFT = 6
HEADS = 8
HEAD_DIM = 128
IDX_HEAD_DIM = 64
TOPK_MAX = 256
MEM_HEADS = 4
MEM_HEAD_DIM = 256
D_FF = 5504
D_FF_PAD = 5632
ROPE_THETA = 10000.0
NORM_EPS = 1e-6
LANES = 128
VMEM_LIMIT = 56 * 1024 * 1024

LOG2_E = 1.4426950408889634
NEG_BIG = -1e30
BIAS_MASKED = -2e30
KEY_TILE = 512
DSA_TQ = 512
DSA_TK = 1024
SB_TAIL_CUTOFF = 104.0


def _cparams(sem):
    return pltpu.CompilerParams(dimension_semantics=sem, vmem_limit_bytes=VMEM_LIMIT)


def _dot(a, b):
    return jnp.dot(a, b, preferred_element_type=F32)


def _dot_nt(a, b):
    return lax.dot_general(a, b, (((1,), (1,)), ((), ())), preferred_element_type=F32)


def _rmsnorm_body(x_ref, g_ref, o_ref):
    x = x_ref[...]
    ms = jnp.mean(x * x, axis=-1, keepdims=True)
    o_ref[...] = ((x * lax.rsqrt(ms + NORM_EPS)) * g_ref[...]).astype(o_ref.dtype)


def _rmsnorm(x, g, tm):
    m, d = x.shape
    return pl.pallas_call(
        _rmsnorm_body,
        out_shape=jax.ShapeDtypeStruct((m, d), BF16),
        grid=(m // tm,),
        in_specs=[pl.BlockSpec((tm, d), lambda i: (i, 0)),
                  pl.BlockSpec((1, d), lambda i: (0, 0))],
        out_specs=pl.BlockSpec((tm, d), lambda i: (i, 0)),
        compiler_params=_cparams(("parallel",)),
        name="rmsnorm",
    )(x, g.reshape(1, d))


def _epi_plain(y, o_ref):
    o_ref[...] = y.astype(o_ref.dtype)


def _epi_headnorm_rope(y, o_ref, g_ref, cos_ref, sin_ref):
    cos = cos_ref[...]
    sin = sin_ref[...]
    heads = [slice(h * HEAD_DIM, (h + 1) * HEAD_DIM) for h in range(y.shape[1] // HEAD_DIM)]
    ms = [jnp.mean(y[:, sl] * y[:, sl], axis=-1, keepdims=True) for sl in heads]
    yn = [(y[:, sl] * lax.rsqrt(m + NORM_EPS)) * g_ref[:, sl] for sl, m in zip(heads, ms)]
    rot = [pltpu.roll(v, HEAD_DIM // 2, 1) for v in yn]
    for sl, v, r in zip(heads, yn, rot):
        o_ref[:, sl] = (v * cos + r * sin).astype(o_ref.dtype)


def _epi_headnorm_mem(y, o_ref, g_ref):
    heads = [slice(h * MEM_HEAD_DIM, (h + 1) * MEM_HEAD_DIM)
             for h in range(y.shape[1] // MEM_HEAD_DIM)]
    ms = [jnp.mean(y[:, sl] * y[:, sl], axis=-1, keepdims=True) for sl in heads]
    for sl, m in zip(heads, ms):
        o_ref[:, sl] = ((y[:, sl] * lax.rsqrt(m + NORM_EPS)) * g_ref[:, sl]).astype(o_ref.dtype)


def _epi_rope_idx(y, o_ref, cos_ref, sin_ref):
    cos = cos_ref[...]
    sin = sin_ref[...]
    lane = lax.broadcasted_iota(I32, cos.shape, 1)
    first_half = (lane % IDX_HEAD_DIM) < (IDX_HEAD_DIM // 2)
    tiles = [slice(t * LANES, (t + 1) * LANES) for t in range(y.shape[1] // LANES)]
    up = [pltpu.roll(y[:, sl], LANES - IDX_HEAD_DIM // 2, 1) for sl in tiles]
    down = [pltpu.roll(y[:, sl], IDX_HEAD_DIM // 2, 1) for sl in tiles]
    for sl, u, d in zip(tiles, up, down):
        o_ref[:, sl] = (y[:, sl] * cos + jnp.where(first_half, u, d) * sin).astype(o_ref.dtype)


def _epi_scale(y, o_ref, *, scale):
    o_ref[...] = (y * scale).astype(o_ref.dtype)


def _epi_gate(y, o_ref, b_ref):
    o_ref[...] = jax.nn.sigmoid(y + b_ref[...]).astype(o_ref.dtype)


def _epi_residual(y, o_ref, r_ref):
    o_ref[...] = (r_ref[...] + y).astype(o_ref.dtype)


def _matmul_body(a_ref, w_ref, *rest, epi, n_extra, nk):
    extras = rest[:n_extra]
    o_ref = rest[n_extra]
    if nk == 1:
        epi(_dot(a_ref[...], w_ref[...]), o_ref, *extras)
        return
    acc_ref = rest[n_extra + 1]
    k = pl.program_id(2)

    @pl.when(k == 0)
    def _():
        acc_ref[...] = jnp.zeros_like(acc_ref)

    acc_ref[...] += _dot(a_ref[...], w_ref[...])

    @pl.when(k == nk - 1)
    def _():
        epi(acc_ref[...], o_ref, *extras)


def _matmul(a, w, epi, extras=(), extra_specs=(), *, out_dtype, tm, tn, tk=None, name):
    m, kdim = a.shape
    n = w.shape[1]
    tk = kdim if tk is None else tk
    nk = kdim // tk
    assert m % tm == 0 and n % tn == 0 and kdim % tk == 0
    scratch = [pltpu.VMEM((tm, tn), F32)] if nk > 1 else []
    return pl.pallas_call(
        functools.partial(_matmul_body, epi=epi, n_extra=len(extras), nk=nk),
        out_shape=jax.ShapeDtypeStruct((m, n), out_dtype),
        grid=(m // tm, n // tn, nk),
        in_specs=[pl.BlockSpec((tm, tk), lambda i, j, k: (i, k)),
                  pl.BlockSpec((tk, tn), lambda i, j, k: (k, j))] + list(extra_specs),
        out_specs=pl.BlockSpec((tm, tn), lambda i, j, k: (i, j)),
        scratch_shapes=scratch,
        compiler_params=_cparams(("parallel", "parallel", "arbitrary")),
        name=name,
    )(a, w, *extras)


def _row_spec(tm, width):
    return pl.BlockSpec((tm, width), lambda i, j, k: (i, 0))


def _col_spec(tn):
    return pl.BlockSpec((1, tn), lambda i, j, k: (0, j))


SB_GROUP = 4


def _sb_body(q_ref, k_ref, v_ref, o_ref, acc_ref, r_ref, *, t, scale):
    i = pl.program_id(1)
    row = lax.broadcasted_iota(I32, (t, t), 0)
    col = lax.broadcasted_iota(I32, (t, t), 1)
    upper = jnp.where(row > col, 1.0, 0.0).astype(BF16)
    visible = col < row

    def block(kb, diagonal):
        start = pl.multiple_of(kb * t, t)
        heads = [slice(g * HEAD_DIM, (g + 1) * HEAD_DIM) for g in range(SB_GROUP)]
        z = [_dot_nt(q_ref[:, sl], k_ref[pl.ds(start, t), sl]) * scale for sl in heads]
        l1p = [jnp.log1p(jnp.exp(-jnp.abs(zz))) for zz in z]
        sp = [jnp.maximum(zz, 0.0) + ll for zz, ll in zip(z, l1p)]
        logsig = [jnp.minimum(zz, 0.0) - ll for zz, ll in zip(z, l1p)]
        if diagonal:
            sp = [jnp.where(visible, x, 0.0) for x in sp]
        hi = [x.astype(BF16) for x in sp]
        lo = [(x - hh.astype(F32)).astype(BF16) for x, hh in zip(sp, hi)]
        within = [_dot(hh, upper) + _dot(ll, upper) for hh, ll in zip(hi, lo)]
        r_old = [r_ref[g] for g in range(SB_GROUP)]
        a = [jnp.exp(ls - (ro + w)) for ls, ro, w in zip(logsig, r_old, within)]
        if diagonal:
            a = [jnp.where(visible, x, 0.0) for x in a]
        r_min = None
        for g, sl in enumerate(heads):
            acc_ref[:, sl] += _dot(a[g].astype(BF16), v_ref[pl.ds(start, t), sl])
            r_new = r_old[g] + jnp.sum(sp[g], axis=-1, keepdims=True)
            r_ref[g] = r_new
            r_head = jnp.min(r_new)
            r_min = r_head if r_min is None else jnp.minimum(r_min, r_head)
        return r_min

    acc_ref[...] = jnp.zeros_like(acc_ref)
    r_ref[...] = jnp.zeros_like(r_ref)
    r_min = block(i, True)

    def cond(c):
        kb, r_min = c
        return jnp.logical_and(kb >= 0, r_min < SB_TAIL_CUTOFF)

    def body(c):
        kb, _ = c
        return kb - 1, block(kb, False)

    lax.while_loop(cond, body, (i - 1, r_min))
    o_ref[...] = acc_ref[...].astype(o_ref.dtype)


def _stick_breaking(p, s, t):
    groups = HEADS // SB_GROUP
    width = SB_GROUP * HEAD_DIM
    return pl.pallas_call(
        functools.partial(_sb_body, t=t, scale=HEAD_DIM ** -0.5),
        out_shape=jax.ShapeDtypeStruct((s, HEADS * HEAD_DIM), BF16),
        grid=(groups, s // t),
        in_specs=[pl.BlockSpec((t, width), lambda h, i: (i, h)),
                  pl.BlockSpec((s, width), lambda h, i: (0, groups + h),
                               pipeline_mode=pl.Buffered(1)),
                  pl.BlockSpec((s, width), lambda h, i: (0, 2 * groups + h),
                               pipeline_mode=pl.Buffered(1))],
        out_specs=pl.BlockSpec((t, width), lambda h, i: (i, h)),
        scratch_shapes=[pltpu.VMEM((t, width), F32), pltpu.VMEM((SB_GROUP, t, 1), F32)],
        compiler_params=_cparams(("parallel", "parallel")),
        name="stick_breaking",
    )(p, p, p)


def _split_index_heads(qi_ref):
    lane = lax.broadcasted_iota(I32, (qi_ref.shape[0], LANES), 1)
    halves = []
    for pair in range(HEADS // 2):
        qp = qi_ref[:, pair * LANES:(pair + 1) * LANES]
        zero = jnp.zeros_like(qp)
        halves.append(jnp.where(lane < IDX_HEAD_DIM, qp, zero))
        halves.append(jnp.where(lane >= IDX_HEAD_DIM, qp, zero))
    return halves


def _index_scores(q_halves, kk, wcols):
    score = None
    for h in range(HEADS):
        term = jnp.maximum(_dot_nt(q_halves[h], kk), 0.0) * wcols[h]
        score = term if score is None else score + term
    return jnp.where(score == 0.0, 0.0, score)


def _admissible(q0, k0, shape):
    qpos = q0 + lax.broadcasted_iota(I32, shape, 0)
    kpos = k0 + lax.broadcasted_iota(I32, shape, 1)
    return (kpos >> CHUNK_SHIFT) <= (qpos >> CHUNK_SHIFT), kpos


def _ordered_key(x):
    b = lax.bitcast_convert_type(x, I32)
    return b ^ ((b >> 31) & jnp.int32(0x7FFFFFFF))


KEY_NEG_INF = int(np.array(-np.inf, np.float32).view(np.int32) ^ 0x7FFFFFFF)
ROW_GROUP = 64


def _key_to_float(k):
    return lax.bitcast_convert_type(k ^ ((k >> 31) & jnp.int32(0x7FFFFFFF)), F32)


def _any(mask):
    return jnp.max(jnp.where(mask, 1.0, 0.0)) > 0.5


def _dsa_select_body(qi_ref, kk_ref, w_ref, bias_ref, key_ref, mx_ref, mn_ref, st_ref,
                     *, t, topk, s_total):
    i = pl.program_id(0)
    n_tiles = (i * t + t + KEY_TILE - 1) // KEY_TILE
    n_all = s_total // KEY_TILE
    lane_tiles = KEY_TILE // LANES
    w = w_ref[...]
    wcols = [w[:, h:h + 1] for h in range(HEADS)]
    q_halves = _split_index_heads(qi_ref)
    (lo_ref, hi_ref, clo_ref, chi_ref, aux_ref, cnt_ref, pos_ref, nonneg_ref) = (
        st_ref.at[k] for k in range(8))
    shape = (t, LANES)
    mx_ref[...] = jnp.full(shape, -jnp.inf, F32)
    mn_ref[...] = jnp.full(shape, jnp.inf, F32)
    pos_ref[...] = jnp.zeros(shape, I32)
    nonneg_ref[...] = jnp.zeros(shape, I32)

    def fill(c, carry, *, masked):
        start = pl.multiple_of(c * KEY_TILE, KEY_TILE)
        score = _index_scores(q_halves, kk_ref[pl.ds(start, KEY_TILE), :], wcols)
        if masked:
            adm, _ = _admissible(i * t, c * KEY_TILE, (t, KEY_TILE))
            low = jnp.where(adm, score, -jnp.inf)
            high = jnp.where(adm, score, jnp.inf)
        else:
            low = high = score
        key_ref[c] = _ordered_key(low)
        mx = mx_ref[...]
        mn = mn_ref[...]
        pos = pos_ref[...]
        nonneg = nonneg_ref[...]
        for sub in range(lane_tiles):
            sl = slice(sub * LANES, (sub + 1) * LANES)
            mx = jnp.maximum(mx, low[:, sl])
            mn = jnp.minimum(mn, high[:, sl])
            pos = pos + jnp.where(low[:, sl] > 0.0, 1, 0).astype(I32)
            nonneg = nonneg + jnp.where(low[:, sl] >= 0.0, 1, 0).astype(I32)
        mx_ref[...] = mx
        mn_ref[...] = mn
        pos_ref[...] = pos
        nonneg_ref[...] = nonneg
        return carry

    n_full = (i * (t // CHUNK) + 1) // (KEY_TILE // CHUNK)
    lax.fori_loop(0, n_full, functools.partial(fill, masked=False), 0)
    lax.fori_loop(n_full, n_tiles, functools.partial(fill, masked=True), 0)

    groups =[slice(g * ROW_GROUP, (g + 1) * ROW_GROUP) for g in range(t // ROW_GROUP)]
    col_in_tile = lax.broadcasted_iota(I32, (ROW_GROUP, LANES), 1)

    def lane_total(x):
        return jnp.broadcast_to(jnp.sum(x, axis=1, keepdims=True), shape)

    def count(pred):
        for rows in groups:
            lo_g = lo_ref[rows, :]
            aux_g = aux_ref[rows, :]

            def tile(c, cnt):
                keys = key_ref[c, rows, :]
                for sub in range(lane_tiles):
                    kcol = c * KEY_TILE + sub * LANES + col_in_tile
                    hit = pred(keys[:, sub * LANES:(sub + 1) * LANES], kcol, lo_g, aux_g)
                    cnt = jnp.where(hit, cnt + 1, cnt)
                return cnt
            cnt_ref[rows, :] = lax.fori_loop(0, n_tiles, tile,
                                             jnp.zeros((ROW_GROUP, LANES), I32))
        return lane_total(cnt_ref[...])

    row_max = jnp.broadcast_to(jnp.max(mx_ref[...], axis=1, keepdims=True), shape)
    row_min = jnp.broadcast_to(jnp.min(mn_ref[...], axis=1, keepdims=True), shape)
    n_pos = lane_total(pos_ref[...])
    n_nonneg = lane_total(nonneg_ref[...])
    qpos = i * t + lax.broadcasted_iota(I32, shape, 0)
    n_adm = ((qpos >> CHUNK_SHIFT) + 1) * CHUNK
    enough = n_adm >= topk

    def active(lo, hi, c_lo):
        return jnp.logical_and(c_lo != topk, hi > lo + 1)

    positive = n_pos >= topk
    negative = n_nonneg < topk
    lo0 = jnp.where(positive, 1, jnp.where(negative, _ordered_key(row_min), 0))
    c_lo0 = jnp.where(positive, n_pos, jnp.where(negative, n_adm, n_nonneg))
    hi0 = jnp.where(positive, _ordered_key(row_max) + 1, jnp.where(negative, 0, 1))
    c_hi0 = jnp.where(positive, 0, jnp.where(negative, n_nonneg, n_pos))
    lo_ref[...] = jnp.where(enough, lo0, KEY_NEG_INF)
    hi_ref[...] = jnp.where(enough, hi0, KEY_NEG_INF + 1)
    clo_ref[...] = jnp.where(enough, c_lo0, topk)
    chi_ref[...] = jnp.where(enough, c_hi0, 0)

    def step(geometric):
        lo = lo_ref[...]
        hi = hi_ref[...]
        c_lo = clo_ref[...]
        act = active(lo, hi, c_lo)
        key_mid = (lo >> 1) + (hi >> 1) + (lo & hi & 1)
        if geometric:
            cand = key_mid
        else:
            mid = _ordered_key(0.5 * _key_to_float(lo) + 0.5 * _key_to_float(hi))
            cand = jnp.where(jnp.logical_and(mid > lo, mid < hi), mid, key_mid)
        aux_ref[...] = cand
        cnt = count(lambda keys, kcol, lo_g, cand_g: keys >= cand_g)
        up = jnp.logical_and(act, cnt >= topk)
        down = jnp.logical_and(act, cnt < topk)
        lo_ref[...] = jnp.where(up, cand, lo)
        clo_ref[...] = jnp.where(up, cnt, c_lo)
        hi_ref[...] = jnp.where(down, cand, hi)
        chi_ref[...] = jnp.where(down, cnt, chi_ref[...])

    def any_active():
        return _any(active(lo_ref[...], hi_ref[...], clo_ref[...])).astype(I32)

    def three_steps(_):
        step(False)
        step(False)
        step(True)
        return any_active()

    lax.while_loop(lambda go: go > 0, three_steps, any_active())

    need = topk - chi_ref[...]
    tie = clo_ref[...] > topk

    def tie_cut():
        def idx_step(b, p):
            cand = p | (jnp.int32(1) << (s_total.bit_length() - 1 - b))
            aux_ref[...] = cand
            cnt = count(lambda keys, kcol, thr_g, cand_g:
                        jnp.logical_and(keys == thr_g, kcol < cand_g))
            return jnp.where(cnt < need, cand, p)
        p = lax.fori_loop(0, s_total.bit_length(), idx_step, jnp.zeros(shape, I32))
        aux_ref[...] = jnp.where(tie, p, s_total)

    aux_ref[...] = jnp.full(shape, s_total, I32)
    has_tie = _any(tie)
    pl.when(has_tie)(tie_cut)

    def emit_masks(with_ties):
        for rows in groups:
            thr = lo_ref[rows, :]
            jcut = aux_ref[rows, :]
            floor = jnp.maximum(thr, KEY_NEG_INF + 1)

            def emit(c, carry):
                keys = key_ref[c, rows, :]
                parts = []
                for sub in range(lane_tiles):
                    ksub = keys[:, sub * LANES:(sub + 1) * LANES]
                    if with_ties:
                        kcol = c * KEY_TILE + sub * LANES + col_in_tile
                        chosen = jnp.logical_or(ksub > thr,
                                                jnp.logical_and(ksub == thr, kcol <= jcut))
                        sel = jnp.logical_and(chosen, ksub > KEY_NEG_INF)
                    else:
                        sel = ksub >= floor
                    parts.append(jnp.where(sel, 0.0, BIAS_MASKED))
                bias_ref[0, c, rows, :] = jnp.concatenate(parts, axis=1).astype(BF16)
                return carry

            lax.fori_loop(0, n_tiles, emit, 0)

    pl.when(has_tie)(functools.partial(emit_masks, True))
    pl.when(jnp.logical_not(has_tie))(functools.partial(emit_masks, False))

    def pad(c, carry):
        bias_ref[0, c] = jnp.full((t, KEY_TILE), BIAS_MASKED, BF16)
        return carry

    lax.fori_loop(n_tiles, n_all, pad, 0)


def _dsa_select(qki, wi, s, t, topk):
    n_all = s // KEY_TILE
    return pl.pallas_call(
        functools.partial(_dsa_select_body, t=t, topk=topk, s_total=s),
        out_shape=jax.ShapeDtypeStruct((s // t, n_all, t, KEY_TILE), BF16),
        grid=(s // t,),
        in_specs=[pl.BlockSpec((t, 4 * LANES), lambda i: (i, 0)),
                  pl.BlockSpec((s, LANES), lambda i: (0, 4)),
                  pl.BlockSpec((t, LANES), lambda i: (i, 0))],
        out_specs=pl.BlockSpec((1, n_all, t, KEY_TILE), lambda i: (i, 0, 0, 0)),
        scratch_shapes=[pltpu.VMEM((n_all, t, KEY_TILE), I32),
                        pltpu.VMEM((t, LANES), F32), pltpu.VMEM((t, LANES), F32),
                        pltpu.VMEM((8, t, LANES), I32)],
        compiler_params=_cparams(("parallel",)),
        name="dsa_select",
    )(qki, qki, wi)


ATT_ROWS = 64


def _dsa_attn_body(qi_of, kj_of, last_of, q_ref, k_ref, v_ref, bias_ref, o_ref,
                   acc_ref, m_ref, l_ref, z_ref, p_ref, b_ref, *, scale):
    step = pl.program_id(0)
    tq, tk = z_ref.shape[1:]

    @pl.when(kj_of[step] == 0)
    def _():
        acc_ref[...] = jnp.zeros_like(acc_ref)
        m_ref[...] = jnp.full_like(m_ref, NEG_BIG)
        l_ref[...] = jnp.zeros_like(l_ref)

    c2 = scale * LOG2_E
    sel_t = bias_ref.shape[2]
    for a in range(bias_ref.shape[0]):
        for b in range(bias_ref.shape[1]):
            b_ref[a * sel_t:(a + 1) * sel_t, b * KEY_TILE:(b + 1) * KEY_TILE] = (
                bias_ref[a, b].astype(F32))

    ones = jnp.ones((tk, HEAD_DIM), BF16)
    chunks = [slice(c * ATT_ROWS, (c + 1) * ATT_ROWS) for c in range(tq // ATT_ROWS)]
    def logits(h):
        sl = slice(h * HEAD_DIM, (h + 1) * HEAD_DIM)
        s = _dot_nt(q_ref[:, sl], k_ref[:, sl]) * c2 + b_ref[...]
        z_ref[h % 2] = s
        m_old = m_ref[h]
        return m_old, jnp.maximum(m_old, jnp.max(s, axis=-1, keepdims=True))

    ahead = logits(0)
    for h in range(HEADS):
        sl = slice(h * HEAD_DIM, (h + 1) * HEAD_DIM)
        buf = h % 2
        m_old, m_new = ahead
        if h + 1 < HEADS:
            ahead = logits(h + 1)
        for rows in chunks:
            p_ref[buf, rows, :] = jnp.exp2(z_ref[buf, rows, :] - m_new[rows]).astype(BF16)
        alpha = jnp.exp2(m_old - m_new)
        pv = _dot(p_ref[buf], jnp.concatenate([v_ref[:, sl], ones], axis=1))
        acc_ref[:, sl] = alpha * acc_ref[:, sl] + pv[:, :HEAD_DIM]
        l_ref[h] = alpha * l_ref[h] + pv[:, HEAD_DIM:]
        m_ref[h] = m_new

    @pl.when(last_of[step] == 1)
    def _():
        for h in range(HEADS):
            sl = slice(h * HEAD_DIM, (h + 1) * HEAD_DIM)
            o_ref[:, sl] = (acc_ref[:, sl] / l_ref[h]).astype(o_ref.dtype)


def _dsa_attention(qkb, p, bias, s, tq, tk):
    sel_t = bias.shape[2]
    qi_of, kj_of, last_of = [], [], []
    for q in range(s // tq):
        n_tiles = (q * tq + tq + tk - 1) // tk
        qi_of += [q] * n_tiles
        kj_of += list(range(n_tiles))
        last_of += [0] * (n_tiles - 1) + [1]
    width = HEADS * HEAD_DIM
    grid_spec = pltpu.PrefetchScalarGridSpec(
        num_scalar_prefetch=3,
        grid=(len(qi_of),),
        in_specs=[
            pl.BlockSpec((tq, width), lambda g, qo, ko, lo: (qo[g], 0)),
            pl.BlockSpec((tk, width), lambda g, qo, ko, lo: (ko[g], 1)),
            pl.BlockSpec((tk, width), lambda g, qo, ko, lo: (ko[g], 3)),
            pl.BlockSpec((tq // sel_t, tk // KEY_TILE, sel_t, KEY_TILE),
                         lambda g, qo, ko, lo: (qo[g], ko[g], 0, 0)),
        ],
        out_specs=pl.BlockSpec((tq, width), lambda g, qo, ko, lo: (qo[g], 0)),
        scratch_shapes=[pltpu.VMEM((tq, width), F32),
                        pltpu.VMEM((HEADS, tq, 1), F32),
                        pltpu.VMEM((HEADS, tq, HEAD_DIM), F32),
                        pltpu.VMEM((2, tq, tk), F32),
                        pltpu.VMEM((2, tq, tk), BF16),
                        pltpu.VMEM((tq, tk), F32)],
    )
    to_i32 = lambda v: jnp.asarray(np.asarray(v, np.int32))
    return pl.pallas_call(
        functools.partial(_dsa_attn_body, scale=HEAD_DIM ** -0.5),
        out_shape=jax.ShapeDtypeStruct((s, width), BF16),
        grid_spec=grid_spec,
        compiler_params=_cparams(("arbitrary",)),
        name="dsa_attention",
    )(to_i32(qi_of), to_i32(kj_of), to_i32(last_of), qkb, qkb, p, bias)


def _mem_attn_body(q_ref, mk_ref, mv_ref, o_ref, *, scale):
    heads = [slice(h * MEM_HEAD_DIM, (h + 1) * MEM_HEAD_DIM) for h in range(MEM_HEADS)]
    sc = [_dot_nt(q_ref[:, sl], mk_ref[:, sl]) * scale for sl in heads]
    e = [jnp.exp(x - jnp.max(x, axis=-1, keepdims=True)) for x in sc]
    p = [x / jnp.sum(x, axis=-1, keepdims=True) for x in e]
    for sl, pp in zip(heads, p):
        o_ref[:, sl] = _dot(pp.astype(BF16), mv_ref[:, sl]).astype(o_ref.dtype)


def _mem_attention(qm, mk, mv, tm):
    s, width = qm.shape
    n_mem = mk.shape[0]
    return pl.pallas_call(
        functools.partial(_mem_attn_body, scale=MEM_HEAD_DIM ** -0.5),
        out_shape=jax.ShapeDtypeStruct((s, width), BF16),
        grid=(s // tm,),
        in_specs=[pl.BlockSpec((tm, width), lambda i: (i, 0)),
                  pl.BlockSpec((n_mem, width), lambda i: (0, 0)),
                  pl.BlockSpec((n_mem, width), lambda i: (0, 0))],
        out_specs=pl.BlockSpec((tm, width), lambda i: (i, 0)),
        compiler_params=_cparams(("parallel",)),
        name="mem_attention",
    )(qm, mk, mv)


def _merge_body(o0_ref, o1_ref, o2_ref, w_ref, g0_ref, g1_ref, g2_ref, out_ref):
    merged = (g0_ref[...] * _dot(o0_ref[...], w_ref[0])
              + g1_ref[...] * _dot(o1_ref[...], w_ref[1])
              + g2_ref[...] * _dot(o2_ref[...], w_ref[2]))
    out_ref[...] = merged.astype(out_ref.dtype)


def _merge(o_sb, o_dsa, o_mem, w_branch, gates, tm, tn):
    s, bw = o_sb.shape
    d = w_branch.shape[2]
    nj = d // tn
    o_spec = pl.BlockSpec((tm, bw), lambda j, i: (i, 0))
    gate_specs = [pl.BlockSpec((tm, tn), functools.partial(lambda j, i, b: (i, b * nj + j), b=b))
                  for b in range(3)]
    return pl.pallas_call(
        _merge_body,
        out_shape=jax.ShapeDtypeStruct((s, d), BF16),
        grid=(nj, s // tm),
        in_specs=[o_spec, o_spec, o_spec,
                  pl.BlockSpec((3, bw, tn), lambda j, i: (0, 0, j))] + gate_specs,
        out_specs=pl.BlockSpec((tm, tn), lambda j, i: (i, j)),
        compiler_params=_cparams(("parallel", "parallel")),
        name="gated_merge",
    )(o_sb, o_dsa, o_mem, w_branch, gates, gates, gates)


def _out_proj_body(m_ref, w_ref, x_ref, g_ref, x1_ref, h2_ref):
    x1 = x_ref[...] + _dot(m_ref[...], w_ref[...])
    x1_ref[...] = x1
    ms = jnp.mean(x1 * x1, axis=-1, keepdims=True)
    h2_ref[...] = ((x1 * lax.rsqrt(ms + NORM_EPS)) * g_ref[...]).astype(h2_ref.dtype)


def _out_proj(merged, w_out, x, g, tm):
    s, d = x.shape
    row = pl.BlockSpec((tm, d), lambda i: (i, 0))
    return pl.pallas_call(
        _out_proj_body,
        out_shape=(jax.ShapeDtypeStruct((s, d), F32), jax.ShapeDtypeStruct((s, d), BF16)),
        grid=(s // tm,),
        in_specs=[row, pl.BlockSpec((d, d), lambda i: (0, 0)), row,
                  pl.BlockSpec((1, d), lambda i: (0, 0))],
        out_specs=(row, row),
        compiler_params=_cparams(("parallel",)),
        name="out_proj",
    )(merged, w_out, x, g.reshape(1, d))


HALO = 8


def _ffn_up_body(h_ref, wg_ref, wv_ref, cwg_ref, cwv_ref, cbg_ref, cbv_ref, o_ref,
                 ug_ref, uv_ref, *, tm):
    i = pl.program_id(1)

    @pl.when(i == 0)
    def _():
        ug_ref[0:HALO, :] = jnp.zeros((HALO, ug_ref.shape[1]), F32)
        uv_ref[0:HALO, :] = jnp.zeros((HALO, uv_ref.shape[1]), F32)

    def conv(u_ref, cw_ref, cb_ref):
        out = cb_ref[...] + u_ref[HALO - 2:HALO - 2 + tm, :] * cw_ref[0:1, :]
        out = out + u_ref[HALO - 1:HALO - 1 + tm, :] * cw_ref[1:2, :]
        return out + u_ref[HALO:HALO + tm, :] * cw_ref[2:3, :]

    h = h_ref[...]
    ug_ref[HALO:HALO + tm, :] = _dot(h, wg_ref[...])
    uv_ref[HALO:HALO + tm, :] = _dot(h, wv_ref[...])
    g = conv(ug_ref, cwg_ref, cbg_ref)
    val = conv(uv_ref, cwv_ref, cbv_ref)
    o_ref[...] = (jax.nn.silu(g) * val).astype(o_ref.dtype)
    ug_ref[0:HALO, :] = ug_ref[tm:tm + HALO, :]
    uv_ref[0:HALO, :] = uv_ref[tm:tm + HALO, :]


def _ffn_up(h2, wg, wv, cwg, cwv, cbg, cbv, tm, tn):
    s, d = h2.shape
    n = wg.shape[1]
    wspec = pl.BlockSpec((d, tn), lambda j, i: (0, j))
    cwspec = pl.BlockSpec((3, tn), lambda j, i: (0, j))
    cbspec = pl.BlockSpec((1, tn), lambda j, i: (0, j))
    return pl.pallas_call(
        functools.partial(_ffn_up_body, tm=tm),
        out_shape=jax.ShapeDtypeStruct((s, n), BF16),
        grid=(n // tn, s // tm),
        in_specs=[pl.BlockSpec((tm, d), lambda j, i: (i, 0)),
                  wspec, wspec, cwspec, cwspec, cbspec, cbspec],
        out_specs=pl.BlockSpec((tm, tn), lambda j, i: (i, j)),
        scratch_shapes=[pltpu.VMEM((tm + HALO, tn), F32), pltpu.VMEM((tm + HALO, tn), F32)],
        compiler_params=_cparams(("parallel", "arbitrary")),
        name="ffn_up_conv_gate",
    )(h2, wg, wv, cwg, cwv, cbg, cbv)


def _rope_tables(positions, head_dim):
    half = head_dim // 2
    inv = ROPE_THETA ** (-jnp.arange(half, dtype=F32) / half)
    ang = positions.astype(F32)[:, None] * inv
    cos = jnp.cos(ang)
    sin = jnp.sin(ang)
    reps = LANES // head_dim
    cos_t = jnp.tile(jnp.concatenate([cos, cos], axis=1), (1, reps))
    sin_t = jnp.tile(jnp.concatenate([-sin, sin], axis=1), (1, reps))
    return cos_t, sin_t


def _pad_cols(a, n):
    return jnp.pad(a, ((0, 0), (0, n - a.shape[1])))


def _layer(x, mem, positions, attn_norm, mem_norm, w_in, b_gate, dsa_q_norm, dsa_k_norm,
           mem_q_norm, mem_k_norm, w_mem_kv, w_branch, w_out, ffn_norm, w_up, conv_w,
           conv_b, w_down):
    s, d = x.shape
    assert d == D_MODEL and s % KEY_TILE == 0
    topk = min(TOPK_MAX, s // 4)
    tm = min(1024, s)
    ta = 256
    hw = HEADS * HEAD_DIM

    c_qa, c_ka, c_va, c_qb, c_kb, c_vb, c_qm = (k * hw for k in range(7))
    c_qi = 7 * hw
    c_ki = c_qi + HEADS * IDX_HEAD_DIM
    c_wi = c_ki + IDX_HEAD_DIM
    c_gl = c_wi + HEADS
    wb16 = w_in.astype(BF16)
    w_plain = jnp.concatenate([wb16[:, c_qa:c_qb], wb16[:, c_vb:c_qm]], axis=1)
    w_dsa_qk = wb16[:, c_qb:c_vb]
    w_qm = wb16[:, c_qm:c_qi]
    w_ki = wb16[:, c_ki:c_wi]
    w_idx = jnp.concatenate([wb16[:, c_qi:c_ki], w_ki, w_ki], axis=1)
    w_wi = _pad_cols(wb16[:, c_wi:c_gl], LANES)
    w_gate = wb16[:, c_gl:]

    cos_d, sin_d = _rope_tables(positions, HEAD_DIM)
    cos_i, sin_i = _rope_tables(positions, IDX_HEAD_DIM)

    h = _rmsnorm(x, attn_norm, min(512, s))

    p = _matmul(h, w_plain, _epi_plain, out_dtype=BF16, tm=tm, tn=1024, name="proj_plain")
    g_dsa = jnp.concatenate([jnp.tile(dsa_q_norm, HEADS), jnp.tile(dsa_k_norm, HEADS)])
    qkb = _matmul(h, w_dsa_qk, _epi_headnorm_rope,
                  (g_dsa.reshape(1, -1), cos_d, sin_d),
                  (_col_spec(1024), _row_spec(tm, LANES), _row_spec(tm, LANES)),
                  out_dtype=BF16, tm=tm, tn=1024, name="proj_dsa_qk")
    qm = _matmul(h, w_qm, _epi_headnorm_mem,
                 (jnp.tile(mem_q_norm, MEM_HEADS).reshape(1, -1),), (_col_spec(1024),),
                 out_dtype=BF16, tm=tm, tn=1024, name="proj_mem_q")
    qki = _matmul(h, w_idx, _epi_rope_idx, (cos_i, sin_i),
                  (_row_spec(tm, LANES), _row_spec(tm, LANES)),
                  out_dtype=BF16, tm=tm, tn=5 * LANES, name="proj_idx_qk")
    wi = _matmul(h, w_wi, functools.partial(_epi_scale, scale=HEADS ** -0.5 * IDX_HEAD_DIM ** -0.5),
                 out_dtype=F32, tm=tm, tn=LANES, name="proj_idx_w")
    gates = _matmul(h, w_gate, _epi_gate, (b_gate.reshape(1, -1),), (_col_spec(1024),),
                    out_dtype=F32, tm=tm, tn=1024, name="proj_gates")

    o_sb = _stick_breaking(p, s, ta)

    bias = _dsa_select(qki, wi, s, ta, topk)
    o_dsa = _dsa_attention(qkb, p, bias, s, min(DSA_TQ, s), min(DSA_TK, s))

    n_mem = mem.shape[0]
    mh = _rmsnorm(mem, mem_norm, n_mem)
    wkv16 = w_mem_kv.astype(BF16)
    mk = _matmul(mh, wkv16[:, :hw], _epi_headnorm_mem,
                 (jnp.tile(mem_k_norm, MEM_HEADS).reshape(1, -1),), (_col_spec(1024),),
                 out_dtype=BF16, tm=n_mem, tn=1024, name="proj_mem_k")
    mv = _matmul(mh, wkv16[:, hw:], _epi_plain, out_dtype=BF16, tm=n_mem, tn=1024,
                 name="proj_mem_v")
    o_mem = _mem_attention(qm, mk, mv, min(512, s))

    merged = _merge(o_sb, o_dsa, o_mem, w_branch.astype(BF16), gates, min(512, s), 1024)
    x1, h2 = _out_proj(merged, w_out.astype(BF16), x, ffn_norm, 256)

    wu16 = w_up.astype(BF16)
    wg = _pad_cols(wu16[:, :D_FF], D_FF_PAD)
    wv = _pad_cols(wu16[:, D_FF:], D_FF_PAD)
    cwg = _pad_cols(conv_w[:, :D_FF], D_FF_PAD)
    cwv = _pad_cols(conv_w[:, D_FF:], D_FF_PAD)
    cbg = _pad_cols(conv_b[None, :D_FF], D_FF_PAD)
    cbv = _pad_cols(conv_b[None, D_FF:], D_FF_PAD)
    act = _ffn_up(h2, wg, wv, cwg, cwv, cbg, cbv, tm, 512)
    wd = jnp.pad(w_down.astype(BF16), ((0, D_FF_PAD - D_FF), (0, 0)))
    return _matmul(act, wd, _epi_residual, (x1,),
                   (pl.BlockSpec((tm, 1024), lambda i, j, k: (i, j)),),
                   out_dtype=F32, tm=tm, tn=1024, tk=D_FF_PAD // 2, name="ffn_down")


def kernel(x, mem, positions, attn_norm, mem_norm, w_in, b_gate, dsa_q_norm, dsa_k_norm,
           mem_q_norm, mem_k_norm, w_mem_kv, w_branch, w_out, ffn_norm, w_up, conv_w,
           conv_b, w_down):
    assert x.shape[0] == 1 and attn_norm.shape[0] == 1
    y = _layer(x[0], mem[0], positions[0], attn_norm[0], mem_norm[0], w_in[0], b_gate[0],
               dsa_q_norm[0], dsa_k_norm[0], mem_q_norm[0], mem_k_norm[0], w_mem_kv[0],
               w_branch[0], w_out[0], ffn_norm[0], w_up[0], conv_w[0], conv_b[0], w_down[0])
    return y[None]
```
